```python
import math
import jax
import jax.numpy as jnp
from jax import lax
import numpy as np

D_MODEL = 1024
BATCH = 4
SEQ = 4096
DEPTH = 1
DEC_BATCH = 32
DEC_SEQ = 4
PAST_LEN = 16384
PAGE_SIZE = 128

ATT_HD = 64
ATT_HEADS = D_MODEL // (2 * ATT_HD)
ATT_VD = 2 * ATT_HD
ATT_QK = ATT_HEADS * 2 * ATT_HD
ATT_V = ATT_HEADS * ATT_VD
Q_BLOCK = 128
N_BUCKETS = 32
MAX_DISTANCE = 128
ATT_EPS = 1e-5
RWKV_HD = 64
RWKV_HEADS = D_MODEL // RWKV_HD
RWKV_W = RWKV_HEADS * RWKV_HD
DECAY_LORA = 64
AAA_LORA = 64
GATE_LORA = 128
RWKV_PROJ = 3 * RWKV_W + DECAY_LORA + AAA_LORA + GATE_LORA
DECAY_SCALE = 0.606531
RWKV_GN_EPS = 64e-5
O_Q = 0
O_K = O_Q + ATT_QK
O_V = O_K + ATT_QK
O_GA = O_V + ATT_V
O_GR = O_GA + D_MODEL
O_RW = O_GR + D_MODEL
N_IN = O_RW + RWKV_PROJ
PEER_HEADS = 8
N_KEYS = 128
N_EXPERTS = N_KEYS * N_KEYS
PEER_QDIM = 256
PEER_HALF = PEER_QDIM // 2
PEER_TOPK = 16
PEER_BLOCK = 256
DN_ALPHA = (2.0 * DEPTH) ** 0.25
DN_BETA = (8.0 * DEPTH) ** -0.25
LN_EPS = 1e-5

kernel_name = 'hybrid_diffattn_rwkv7_peer_step'


def layer_norm(x, g, b):
    xf = x.astype(jnp.float32)
    mu = jnp.mean(xf, axis=-1, keepdims=True)
    var = jnp.mean(jnp.square(xf - mu), axis=-1, keepdims=True)
    return ((xf - mu) * lax.rsqrt(var + LN_EPS)).astype(x.dtype) * g + b


def t5_bucket(rel):
    n = jnp.maximum(rel, 0)
    max_exact = N_BUCKETS // 2
    nf = jnp.maximum(n, 1).astype(jnp.float32)
    large = max_exact + (jnp.log(nf / max_exact) / math.log(MAX_DISTANCE / max_exact)
                         * (N_BUCKETS - max_exact)).astype(jnp.int32)
    large = jnp.minimum(large, N_BUCKETS - 1)
    return jnp.where(n < max_exact, n, large)


def diff_attend(q, k, v, q_pos, k_pos, lam, rel_bias):
    s = jnp.einsum('bqhcd,bkhcd->bhcqk', q, k).astype(jnp.float32) * (ATT_HD ** -0.5)
    bias = jnp.transpose(rel_bias[t5_bucket(q_pos[:, None] - k_pos[None, :])], (2, 0, 1))
    s = s + bias.astype(jnp.float32)[None, :, None]
    s = jnp.where(k_pos[None, :] <= q_pos[:, None], s, -1e30)
    p = jax.nn.softmax(s, axis=-1).astype(v.dtype)
    o = jnp.einsum('bhcqk,bkhv->bqhcv', p, v)
    return o[:, :, :, 0] - lam * o[:, :, :, 1]


def prompt_attention(q, k, v, lam, rel_bias):
    B, T = q.shape[0], q.shape[1]
    k_pos = jnp.arange(T)

    def block(i):
        start = i * Q_BLOCK
        qb = lax.dynamic_slice_in_dim(q, start, Q_BLOCK, axis=1)
        q_pos = start + jnp.arange(Q_BLOCK)
        return diff_attend(qb, k, v, q_pos, k_pos, lam, rel_bias)

    o = lax.map(block, jnp.arange(T // Q_BLOCK))
    return jnp.moveaxis(o, 0, 1).reshape(B, T, ATT_HEADS, ATT_VD)


def sample_attention(q, k, v, lam, cache_k, cache_v, page_table, rel_bias):
    past = page_table.shape[1] * PAGE_SIZE
    S = q.shape[1]
    q_pos = past + jnp.arange(S)
    k_pos = jnp.arange(past + S)

    def one(args):
        pt, qb, kb, vb = args
        kp = cache_k[pt].reshape(past, ATT_HEADS, 2, ATT_HD)
        vp = cache_v[pt].reshape(past, ATT_HEADS, ATT_VD)
        kf = jnp.concatenate([kp, kb], axis=0)[None]
        vf = jnp.concatenate([vp, vb], axis=0)[None]
        return diff_attend(qb[None], kf, vf, q_pos, k_pos, lam, rel_bias)[0]

    return lax.map(one, (page_table, q, k, v))


def wkv_scan(S0, r, w, k, v, kk, a):
    def step(S, inp):
        r_t, w_t, k_t, v_t, kk_t, a_t = inp
        sa = jnp.einsum('bhvk,bhk->bhv', S, kk_t)
        S = (S * w_t[:, :, None, :] - sa[..., None] * (kk_t * a_t)[:, :, None, :]
             + v_t[..., None] * k_t[:, :, None, :])
        return S, jnp.einsum('bhvk,bhk->bhv', S, r_t)

    xs = tuple(jnp.moveaxis(t, 1, 0) for t in (r, w, k, v, kk, a))
    S, o = lax.scan(step, S0, xs)
    return S, jnp.moveaxis(o, 0, 1)


def rwkv_branch(p, shift0, S0, mu, w0, w2, a0, a2, g2, kk_s, ka, rk, ln_g, ln_b):
    B, T, _ = p.shape
    prev = jnp.concatenate([shift0[:, None, :], p[:, :-1, :]], axis=1)
    xs = p + (prev - p) * mu
    r, k, v, wd, ad, gd = jnp.split(
        xs, [RWKV_W, 2 * RWKV_W, 3 * RWKV_W, 3 * RWKV_W + DECAY_LORA,
             3 * RWKV_W + DECAY_LORA + AAA_LORA], axis=-1)
    decay = jnp.exp(-DECAY_SCALE * jax.nn.sigmoid((w0 + jnp.tanh(wd) @ w2).astype(jnp.float32)))
    a = jax.nn.sigmoid(a0 + ad @ a2)
    g = jax.nn.sigmoid(gd) @ g2
    hd = lambda t: t.reshape(B, T, RWKV_HEADS, RWKV_HD).astype(jnp.float32)
    kk = hd(k * kk_s)
    kk = kk / jnp.maximum(jnp.sqrt(jnp.sum(kk * kk, axis=-1, keepdims=True)), 1e-12)
    k = k * (1 + (a - 1) * ka)
    rf, kf, vf = hd(r), hd(k), hd(v)
    S, o = wkv_scan(S0.astype(jnp.float32), rf, hd(decay), kf, vf, kk, hd(a))
    mean = jnp.mean(o, axis=-1, keepdims=True)
    var = jnp.mean(jnp.square(o - mean), axis=-1, keepdims=True)
    o = ((o - mean) * lax.rsqrt(var + RWKV_GN_EPS)).reshape(B, T, RWKV_W).astype(p.dtype) * ln_g + ln_b
    bonus = (jnp.sum(rf * kf * rk.reshape(RWKV_HEADS, RWKV_HD).astype(jnp.float32), axis=-1, keepdims=True)
             * vf).reshape(B, T, RWKV_W).astype(p.dtype)
    return (o + bonus) * g, S.astype(S0.dtype), p[:, -1, :]


def peer_ffn(x, wq, keys, u, v):
    B, T, D = x.shape
    n = B * T
    nb = -(-n // PEER_BLOCK)
    xt = jnp.pad(x.reshape(n, D), ((0, nb * PEER_BLOCK - n), (0, 0)))

    def block(xb):
        q = (xb @ wq).reshape(-1, PEER_HEADS, 2, PEER_HALF)
        s = jnp.einsum('thcd,hcnd->thcn', q, keys).astype(jnp.float32)
        s1, i1 = lax.top_k(s[:, :, 0], PEER_TOPK)
        s2, i2 = lax.top_k(s[:, :, 1], PEER_TOPK)
        cand = (s1[..., :, None] + s2[..., None, :]).reshape(-1, PEER_HEADS, PEER_TOPK * PEER_TOPK)
        cidx = (i1[..., :, None] * N_KEYS + i2[..., None, :]).reshape(-1, PEER_HEADS, PEER_TOPK * PEER_TOPK)
        sc, pos = lax.top_k(cand, PEER_TOPK)
        idx = jnp.take_along_axis(cidx, pos, axis=-1)
        gate = jax.nn.softmax(sc, axis=-1).astype(xb.dtype)
        hid = jax.nn.gelu(jnp.einsum('thkd,td->thk', u[idx], xb), approximate=False)
        return jnp.einsum('thk,thkd->td', gate * hid, v[idx])

    out = lax.map(block, xt.reshape(nb, PEER_BLOCK, D))
    return out.reshape(nb * PEER_BLOCK, D)[:n].reshape(B, T, D)


def decoder_layer(x, attend, shift0, S0, lp, lambda_init):
    B, T, _ = x.shape
    proj = x @ lp['w_in']
    q, k, v, ga, gr, p = jnp.split(proj, [O_K, O_V, O_GA, O_GR, O_RW], axis=-1)
    q = q.reshape(B, T, ATT_HEADS, 2, ATT_HD)
    k = k.reshape(B, T, ATT_HEADS, 2, ATT_HD)
    v = v.reshape(B, T, ATT_HEADS, ATT_VD)
    lq = lp['lam_qk']
    lam = jnp.exp(jnp.sum(lq[0] * lq[1])) - jnp.exp(jnp.sum(lq[2] * lq[3])) + lambda_init
    o = attend(q, k, v, lam)
    of = o.astype(jnp.float32)
    o = (of * lax.rsqrt(jnp.mean(of * of, axis=-1, keepdims=True) + ATT_EPS)).astype(x.dtype)
    att_out = o.reshape(B, T, ATT_V) * lp['attn_norm_g'] * (1.0 - lambda_init)
    rw_out, S_new, shift_new = rwkv_branch(
        p, shift0, S0, lp['shift_mu'], lp['rwkv_w0'], lp['rwkv_w2'], lp['rwkv_a0'], lp['rwkv_a2'],
        lp['rwkv_g2'], lp['rwkv_kk'], lp['rwkv_ka'], lp['rwkv_rk'], lp['rwkv_ln_g'], lp['rwkv_ln_b'])
    mixed = (jax.nn.sigmoid(ga) * att_out + jax.nn.sigmoid(gr) * rw_out) @ lp['w_out']
    h = layer_norm(DN_ALPHA * x + mixed, lp['ln1_g'], lp['ln1_b'])
    ff = peer_ffn(h, lp['peer_wq'], lp['peer_keys'], lp['peer_u'], lp['peer_v'])
    y = layer_norm(DN_ALPHA * h + ff, lp['ln2_g'], lp['ln2_b'])
    return (y, k.reshape(B, T, ATT_HEADS, 2 * ATT_HD), v, S_new, shift_new)


def setup_inputs(seed: int = 0) -> dict:
    key = jax.random.key(seed)
    ks = iter(jax.random.split(key, 40))
    nrm = lambda shape, scale: scale * jax.random.normal(next(ks), shape, jnp.float32)
    n_pages = PAST_LEN // PAGE_SIZE
    n_phys = (DEC_BATCH * n_pages * 5) // 4
    perm = jax.random.permutation(next(ks), n_phys)
    page_table = perm[: DEC_BATCH * n_pages].reshape(DEC_BATCH, n_pages).astype(jnp.int32)
    x_prompt = nrm((BATCH, SEQ, D_MODEL), 1.0)
    x_sample = nrm((DEC_BATCH, DEC_SEQ, D_MODEL), 1.0)
    cache_k = nrm((DEPTH, n_phys, PAGE_SIZE, ATT_HEADS, 2 * ATT_HD), 1.0)
    cache_v = nrm((DEPTH, n_phys, PAGE_SIZE, ATT_HEADS, ATT_VD), DN_BETA)
    state_wkv = nrm((DEPTH, DEC_BATCH, RWKV_HEADS, RWKV_HD, RWKV_HD), 0.3)
    state_shift = nrm((DEPTH, DEC_BATCH, RWKV_PROJ), 1.0)
    col_scale = (jnp.ones((N_IN,), jnp.float32)
                 .at[O_V:O_V + ATT_V].set(DN_BETA)
                 .at[O_RW + 2 * RWKV_W:O_RW + 3 * RWKV_W].set(DN_BETA))
    w_in = nrm((DEPTH, D_MODEL, N_IN), D_MODEL ** -0.5) * col_scale
    w_out = nrm((DEPTH, D_MODEL, D_MODEL), DN_BETA * D_MODEL ** -0.5)
    lam_qk = nrm((DEPTH, 4, ATT_HD), 0.1)
    attn_norm_g = 1.0 + nrm((DEPTH, ATT_V), 0.05)
    shift_mu = jax.random.uniform(next(ks), (DEPTH, RWKV_PROJ), jnp.float32)
    rwkv_w0 = nrm((DEPTH, RWKV_W), 0.5)
    rwkv_w2 = nrm((DEPTH, DECAY_LORA, RWKV_W), 0.1 * DECAY_LORA ** -0.5)
    rwkv_a0 = nrm((DEPTH, RWKV_W), 0.1)
    rwkv_a2 = nrm((DEPTH, AAA_LORA, RWKV_W), 0.1 * AAA_LORA ** -0.5)
    rwkv_g2 = nrm((DEPTH, GATE_LORA, RWKV_W), GATE_LORA ** -0.5)
    rwkv_kk = 0.85 + nrm((DEPTH, RWKV_W), 0.05)
    rwkv_ka = 1.0 + nrm((DEPTH, RWKV_W), 0.05)
    rwkv_rk = nrm((DEPTH, RWKV_W), 0.1)
    rwkv_ln_g = 1.0 + nrm((DEPTH, RWKV_W), 0.05)
    rwkv_ln_b = nrm((DEPTH, RWKV_W), 0.02)
    ln1_g = 1.0 + nrm((DEPTH, D_MODEL), 0.05)
    ln1_b = nrm((DEPTH, D_MODEL), 0.02)
    ln2_g = 1.0 + nrm((DEPTH, D_MODEL), 0.05)
    ln2_b = nrm((DEPTH, D_MODEL), 0.02)
    peer_wq = nrm((DEPTH, D_MODEL, PEER_HEADS * PEER_QDIM), D_MODEL ** -0.5)
    peer_keys = nrm((DEPTH, PEER_HEADS, 2, N_KEYS, PEER_HALF), PEER_HALF ** -0.5)
    peer_u = nrm((DEPTH, N_EXPERTS, D_MODEL), D_MODEL ** -0.5)
    peer_v = nrm((DEPTH, N_EXPERTS, D_MODEL), DN_BETA * 0.5)
    rel_bias = nrm((N_BUCKETS, ATT_HEADS), 0.5)
    return {'x_prompt': x_prompt, 'x_sample': x_sample, 'cache_k': cache_k, 'cache_v': cache_v,
            'state_wkv': state_wkv, 'state_shift': state_shift, 'page_table': page_table,
            'w_in': w_in, 'w_out': w_out, 'lam_qk': lam_qk, 'attn_norm_g': attn_norm_g,
            'shift_mu': shift_mu, 'rwkv_w0': rwkv_w0, 'rwkv_w2': rwkv_w2, 'rwkv_a0': rwkv_a0,
            'rwkv_a2': rwkv_a2, 'rwkv_g2': rwkv_g2, 'rwkv_kk': rwkv_kk, 'rwkv_ka': rwkv_ka,
            'rwkv_rk': rwkv_rk, 'rwkv_ln_g': rwkv_ln_g, 'rwkv_ln_b': rwkv_ln_b,
            'ln1_g': ln1_g, 'ln1_b': ln1_b, 'ln2_g': ln2_g, 'ln2_b': ln2_b,
            'peer_wq': peer_wq, 'peer_keys': peer_keys, 'peer_u': peer_u, 'peer_v': peer_v,
            'rel_bias': rel_bias}


def reference(x_prompt, x_sample, cache_k, cache_v, state_wkv, state_shift, page_table,
              w_in, w_out, lam_qk, attn_norm_g, shift_mu, rwkv_w0, rwkv_w2, rwkv_a0, rwkv_a2,
              rwkv_g2, rwkv_kk, rwkv_ka, rwkv_rk, rwkv_ln_g, rwkv_ln_b, ln1_g, ln1_b, ln2_g, ln2_b,
              peer_wq, peer_keys, peer_u, peer_v, rel_bias):
    B = x_prompt.shape[0]
    yp, ys = x_prompt, x_sample
    kps, vps, kss, vss, Sps, Sss, shps, shss = [], [], [], [], [], [], [], []
    for l in range(DEPTH):
        lp = {'w_in': w_in[l], 'w_out': w_out[l], 'lam_qk': lam_qk[l], 'attn_norm_g': attn_norm_g[l],
              'shift_mu': shift_mu[l], 'rwkv_w0': rwkv_w0[l], 'rwkv_w2': rwkv_w2[l],
              'rwkv_a0': rwkv_a0[l], 'rwkv_a2': rwkv_a2[l], 'rwkv_g2': rwkv_g2[l],
              'rwkv_kk': rwkv_kk[l], 'rwkv_ka': rwkv_ka[l], 'rwkv_rk': rwkv_rk[l],
              'rwkv_ln_g': rwkv_ln_g[l], 'rwkv_ln_b': rwkv_ln_b[l],
              'ln1_g': ln1_g[l], 'ln1_b': ln1_b[l], 'ln2_g': ln2_g[l], 'ln2_b': ln2_b[l],
              'peer_wq': peer_wq[l], 'peer_keys': peer_keys[l], 'peer_u': peer_u[l], 'peer_v': peer_v[l]}
        lambda_init = 0.8 - 0.6 * math.exp(-0.3 * l)
        ck, cv = cache_k[l], cache_v[l]
        prompt_attend = lambda q, k, v, lam: prompt_attention(q, k, v, lam, rel_bias)
        sample_attend = lambda q, k, v, lam: sample_attention(q, k, v, lam, ck, cv, page_table, rel_bias)
        shift0 = jnp.zeros((B, RWKV_PROJ), x_prompt.dtype)
        S0 = jnp.zeros((B, RWKV_HEADS, RWKV_HD, RWKV_HD), x_prompt.dtype)
        yp, kp, vp, Sp, shp = decoder_layer(yp, prompt_attend, shift0, S0, lp, lambda_init)
        ys, ksm, vsm, Ss, shs = decoder_layer(ys, sample_attend, state_shift[l], state_wkv[l], lp, lambda_init)
        kps.append(kp); vps.append(vp); kss.append(ksm); vss.append(vsm)
        Sps.append(Sp); Sss.append(Ss); shps.append(shp); shss.append(shs)
    k_prompt = jnp.stack(kps, axis=0)
    v_prompt = jnp.stack(vps, axis=0)
    k_sample = jnp.stack(kss, axis=0)
    v_sample = jnp.stack(vss, axis=0)
    wkv_prompt = jnp.stack(Sps, axis=0)
    wkv_sample = jnp.stack(Sss, axis=0)
    shift_prompt = jnp.stack(shps, axis=0)
    shift_sample = jnp.stack(shss, axis=0)
    return (yp, ys, k_prompt, v_prompt, k_sample, v_sample, wkv_prompt, wkv_sample, shift_prompt, shift_sample)
```

```python
import functools
import math

import jax
import jax.numpy as jnp
from jax import lax
from jax.experimental import pallas as pl
from jax.experimental.pallas import tpu as pltpu

D_MODEL = 1024
PAGE_SIZE = 128
ATT_HD = 64
ATT_HEADS = D_MODEL // (2 * ATT_HD)
ATT_VD = 2 * ATT_HD
ATT_QK = ATT_HEADS * 2 * ATT_HD
ATT_V = ATT_HEADS * ATT_VD
N_BUCKETS = 32
MAX_DISTANCE = 128
ATT_EPS = 1e-5
RWKV_HD = 64
RWKV_HEADS = D_MODEL // RWKV_HD
RWKV_W = RWKV_HEADS * RWKV_HD
DECAY_LORA = 64
AAA_LORA = 64
GATE_LORA = 128
RWKV_PROJ = 3 * RWKV_W + DECAY_LORA + AAA_LORA + GATE_LORA
DECAY_SCALE = 0.606531
RWKV_GN_EPS = 64e-5
O_K = ATT_QK
O_V = O_K + ATT_QK
O_GA = O_V + ATT_V
O_GR = O_GA + D_MODEL
O_RW = O_GR + D_MODEL
N_IN = O_RW + RWKV_PROJ
PEER_HEADS = 8
N_KEYS = 128
N_EXPERTS = N_KEYS * N_KEYS
PEER_QDIM = 256
PEER_HALF = PEER_QDIM // 2
PEER_TOPK = 16
DEPTH = 1
DN_ALPHA = (2.0 * DEPTH) ** 0.25
LN_EPS = 1e-5
LAMBDA_INIT = 0.8 - 0.6 * math.exp(-0.3 * 0)

V7X_LANES = 128
V7X_SUBLANES = 8
V7X_VMEM_LIMIT_BYTES = 56 * 1024 * 1024

NEG_INF = -1e30
BF16 = jnp.bfloat16
F32 = jnp.float32

NT_DIMS = (((1,), (1,)), ((), ()))
TN_DIMS = (((0,), (0,)), ((), ()))


def _cparams(*sem):
    return pltpu.CompilerParams(dimension_semantics=sem, vmem_limit_bytes=V7X_VMEM_LIMIT_BYTES)


def _dot(a, b):
    return jnp.dot(a, b, preferred_element_type=F32)


def _dot_nt(a, b):
    return lax.dot_general(a, b, NT_DIMS, preferred_element_type=F32)


def _dot_tn(a, b):
    return lax.dot_general(a, b, TN_DIMS, preferred_element_type=F32)


def _split_dot(x, w_bf16):
    hi = x.astype(BF16)
    lo = (x - hi.astype(F32)).astype(BF16)
    return _dot(hi, w_bf16) + _dot(lo, w_bf16)


_P_CHUNKS = ((0, 1024), (1024, 2048), (2048, 3072), (3072, RWKV_PROJ))


def _in_proj_kernel(x_ref, w_ref, q16_ref, k32_ref, k16_ref, v32_ref, v16_ref, gates_ref, p_ref):
    x = x_ref[...].astype(BF16)

    def mm(lo, hi):
        return _dot(x, w_ref[:, lo:hi])

    q16_ref[...] = (mm(0, O_K) * (ATT_HD ** -0.5)).astype(BF16)
    k = mm(O_K, O_V)
    k32_ref[...] = k
    k16_ref[...] = k.astype(BF16)
    v = mm(O_V, O_GA)
    v32_ref[...] = v
    v16_ref[...] = v.astype(BF16)
    gates_ref[:, 0:D_MODEL] = mm(O_GA, O_GR)
    gates_ref[:, D_MODEL:2 * D_MODEL] = mm(O_GR, O_RW)
    for lo, hi in _P_CHUNKS:
        p_ref[:, lo:hi] = mm(O_RW + lo, O_RW + hi)


def _in_proj(x2d, w_bf16, tm):
    n = x2d.shape[0]
    row = lambda i: (i, 0)
    full = lambda i: (0, 0)
    out_shape = (
        jax.ShapeDtypeStruct((n, ATT_QK), BF16),
        jax.ShapeDtypeStruct((n, ATT_QK), F32),
        jax.ShapeDtypeStruct((n, ATT_QK), BF16),
        jax.ShapeDtypeStruct((n, ATT_V), F32),
        jax.ShapeDtypeStruct((n, ATT_V), BF16),
        jax.ShapeDtypeStruct((n, 2 * D_MODEL), F32),
        jax.ShapeDtypeStruct((n, RWKV_PROJ), F32),
    )
    out_specs = tuple(pl.BlockSpec((tm, s.shape[1]), row) for s in out_shape)
    return pl.pallas_call(
        _in_proj_kernel,
        grid=(n // tm,),
        in_specs=[pl.BlockSpec((tm, D_MODEL), row),
                  pl.BlockSpec((D_MODEL, N_IN), full, pipeline_mode=pl.Buffered(1))],
        out_specs=out_specs,
        out_shape=out_shape,
        compiler_params=_cparams("parallel"),
        name="in_proj",
    )(x2d, w_bf16)


def _t5_bucket(rel):
    n = jnp.maximum(rel, 0)
    max_exact = N_BUCKETS // 2
    nf = jnp.maximum(n, 1).astype(F32)
    large = max_exact + (jnp.log(nf / max_exact) / math.log(MAX_DISTANCE / max_exact)
                         * (N_BUCKETS - max_exact)).astype(jnp.int32)
    large = jnp.minimum(large, N_BUCKETS - 1)
    return jnp.where(n < max_exact, n, large)


def _shifted_bias(rel_bias, dist):
    far = rel_bias[_t5_bucket(jnp.int32(2 * MAX_DISTANCE))]
    b = rel_bias[_t5_bucket(dist)] - far
    return jnp.moveaxis(b, -1, 0)


def _lambda_value(lq):
    s01 = jnp.sum(lq[0:1] * lq[1:2], axis=-1, keepdims=True)
    s23 = jnp.sum(lq[2:3] * lq[3:4], axis=-1, keepdims=True)
    return jnp.exp(s01) - jnp.exp(s23) + LAMBDA_INIT


def _rms_scale(o):
    return o * lax.rsqrt(jnp.mean(o * o, axis=-1, keepdims=True) + ATT_EPS)


def _prompt_attn_kernel(q_ref, k_ref, v_ref, bias_ref, lam_ref, g_ref, o_ref, m_ref, l_ref, acc_ref, *, tq):
    i = pl.program_id(2)
    lane = lax.broadcasted_iota(jnp.int32, (tq, 2 * ATT_HD), 1)
    q = q_ref[0]
    zero = jnp.zeros_like(q)
    qm = (jnp.where(lane < ATT_HD, q, zero), jnp.where(lane >= ATT_HD, q, zero))

    m_ref[...] = jnp.full(m_ref.shape, NEG_INF, F32)
    l_ref[...] = jnp.zeros(l_ref.shape, F32)
    acc_ref[...] = jnp.zeros(acc_ref.shape, F32)

    def step(j, add):
        start = pl.multiple_of(j * tq, tq)
        kb = k_ref[0, pl.ds(start, tq), :]
        vb = v_ref[0, pl.ds(start, tq), :]
        for c in range(2):
            s = _dot_nt(qm[c], kb)
            if add is not None:
                s = s + add
            m_old = m_ref[c]
            m_new = jnp.maximum(m_old, jnp.max(s, axis=-1, keepdims=True))
            alpha = jnp.exp(m_old - m_new)
            p = jnp.exp(s - m_new)
            l_ref[c] = alpha * l_ref[c] + jnp.sum(p, axis=-1, keepdims=True)
            acc_ref[c] = alpha * acc_ref[c] + _dot(p.astype(BF16), vb)
            m_ref[c] = m_new

    def far_body(j, carry):
        step(j, None)
        return carry

    lax.fori_loop(0, jnp.maximum(i - 1, 0), far_body, 0)

    @pl.when(i > 0)
    def _():
        step(i - 1, bias_ref[0, 1])

    step(i, bias_ref[0, 0])

    lam = _lambda_value(lam_ref[...])
    o = acc_ref[0] / l_ref[0] - lam * (acc_ref[1] / l_ref[1])
    o_ref[0] = _rms_scale(o) * (g_ref[...] * (1.0 - LAMBDA_INIT))


def _prompt_attention(q16, k16, v16, rel_bias, lam_qk, attn_norm_g, tq):
    B, T, _ = q16.shape
    r = jnp.arange(tq, dtype=jnp.int32)
    diff = r[:, None] - r[None, :]
    diag = jnp.where(diff >= 0, _shifted_bias(rel_bias, diff), NEG_INF)
    sub = _shifted_bias(rel_bias, diff + tq)
    bias = jnp.stack([diag, sub], axis=1).astype(F32)
    kern = functools.partial(_prompt_attn_kernel, tq=tq)
    return pl.pallas_call(
        kern,
        grid=(B, ATT_HEADS, T // tq),
        in_specs=[
            pl.BlockSpec((1, tq, 2 * ATT_HD), lambda b, h, i: (b, i, h)),
            pl.BlockSpec((1, T, 2 * ATT_HD), lambda b, h, i: (b, 0, h)),
            pl.BlockSpec((1, T, ATT_VD), lambda b, h, i: (b, 0, h)),
            pl.BlockSpec((1, 2, tq, tq), lambda b, h, i: (h, 0, 0, 0)),
            pl.BlockSpec((4, ATT_HD), lambda b, h, i: (0, 0)),
            pl.BlockSpec((1, ATT_VD), lambda b, h, i: (0, h)),
        ],
        out_specs=pl.BlockSpec((1, tq, ATT_VD), lambda b, h, i: (b, i, h)),
        out_shape=jax.ShapeDtypeStruct((B, T, ATT_V), F32),
        scratch_shapes=[pltpu.VMEM((2, tq, 1), F32), pltpu.VMEM((2, tq, 1), F32),
                        pltpu.VMEM((2, tq, ATT_VD), F32)],
        compiler_params=_cparams("parallel", "parallel", "arbitrary"),
        name="prompt_attention",
    )(q16, k16, v16, bias, lam_qk, attn_norm_g.reshape(1, ATT_V))


def _sample_attn_kernel(pt_ref, q_ref, plain_ref, last_ref, new_ref, knew_ref, vnew_ref, lam_ref, g_ref,
                        *rest, pages_per_step):
    k_refs = rest[:pages_per_step]
    v_refs = rest[pages_per_step:2 * pages_per_step]
    o_ref, m_ref, l_ref, acc_ref = rest[2 * pages_per_step:]
    j = pl.program_id(1)
    last = pl.num_programs(1) - 1
    q = q_ref[0]

    @pl.when(j == 0)
    def _():
        m_ref[...] = jnp.full(m_ref.shape, NEG_INF, F32)
        l_ref[...] = jnp.zeros(l_ref.shape, F32)
        acc_ref[...] = jnp.zeros(acc_ref.shape, F32)

    def update(s, v2d):
        m_old = m_ref[...]
        m_new = jnp.maximum(m_old, jnp.max(s, axis=-1, keepdims=True))
        alpha = jnp.exp(m_old - m_new)
        p = jnp.exp(s - m_new)
        l_ref[...] = alpha * l_ref[...] + jnp.sum(p, axis=-1, keepdims=True)
        acc_ref[...] = alpha * acc_ref[...] + _dot(p.astype(BF16), v2d)
        m_ref[...] = m_new

    for g in range(pages_per_step):
        k2d = k_refs[g][...].reshape(PAGE_SIZE * ATT_HEADS, 2 * ATT_HD).astype(BF16)
        v2d = v_refs[g][...].reshape(PAGE_SIZE * ATT_HEADS, ATT_VD).astype(BF16)
        add = last_ref[0] if g == pages_per_step - 1 else plain_ref[...]
        update(_dot_nt(q, k2d) + add, v2d)

    @pl.when(j == last)
    def _():
        n_new = knew_ref.shape[1]
        k2d = knew_ref[0].reshape(n_new * ATT_HEADS, 2 * ATT_HD).astype(BF16)
        v2d = vnew_ref[0].reshape(n_new * ATT_HEADS, ATT_VD).astype(BF16)
        update(_dot_nt(q, k2d) + new_ref[...], v2d)
        half = acc_ref.shape[0] // 2
        o = acc_ref[...] / l_ref[...]
        lam = _lambda_value(lam_ref[...])
        diff = o[0:half] - lam * o[half:2 * half]
        o_ref[0] = _rms_scale(diff) * (g_ref[...] * (1.0 - LAMBDA_INIT))


def _sample_attention(q16, k32, v32, cache_k, cache_v, layer, page_table, rel_bias, lam_qk, attn_norm_g,
                      pages_per_step):
    DB, S, _ = q16.shape
    n_pages = page_table.shape[1]
    past = n_pages * PAGE_SIZE
    H = ATT_HEADS
    rows = 2 * S * H
    q5 = q16.reshape(DB, S, H, 2, ATT_HD)
    qrows = jnp.stack([jnp.concatenate([q5[:, :, :, 0], jnp.zeros_like(q5[:, :, :, 0])], axis=-1),
                       jnp.concatenate([jnp.zeros_like(q5[:, :, :, 1]), q5[:, :, :, 1]], axis=-1)], axis=1)
    qrows = qrows.reshape(DB, rows, 2 * ATT_HD)
    row_s = (jnp.arange(rows, dtype=jnp.int32) // H) % S
    row_h = jnp.arange(rows, dtype=jnp.int32) % H
    col_t = jnp.arange(PAGE_SIZE * H, dtype=jnp.int32) // H
    col_h = jnp.arange(PAGE_SIZE * H, dtype=jnp.int32) % H
    same_head = row_h[:, None] == col_h[None, :]
    plain = jnp.where(same_head, 0.0, NEG_INF).astype(F32)
    dist_last = PAGE_SIZE + row_s[:, None] - col_t[None, :]
    b_last = _shifted_bias(rel_bias, dist_last)
    b_last = jnp.take_along_axis(b_last, jnp.broadcast_to(row_h[None, :, None], (1, rows, PAGE_SIZE * H)), axis=0)[0]
    last_tile = jnp.where(same_head, b_last, NEG_INF).astype(F32)
    tiles = jnp.stack([plain, last_tile], axis=0)
    ncol_t = jnp.arange(S * H, dtype=jnp.int32) // H
    ncol_h = jnp.arange(S * H, dtype=jnp.int32) % H
    dist_new = row_s[:, None] - ncol_t[None, :]
    b_new = _shifted_bias(rel_bias, dist_new)
    b_new = jnp.take_along_axis(b_new, jnp.broadcast_to(row_h[None, :, None], (1, rows, S * H)), axis=0)[0]
    ok_new = (row_h[:, None] == ncol_h[None, :]) & (dist_new >= 0)
    new_tile = jnp.where(ok_new, b_new, NEG_INF).astype(F32)
    g_tile = jnp.tile(attn_norm_g.reshape(H, ATT_VD), (S, 1))

    n_steps = n_pages // pages_per_step
    kern = functools.partial(_sample_attn_kernel, pages_per_step=pages_per_step)

    def page_spec(g):
        return pl.BlockSpec((None, None, PAGE_SIZE, H, 2 * ATT_HD),
                            lambda b, j, pt, g=g: (layer, pt[b, j * pages_per_step + g], 0, 0, 0))

    grid_spec = pltpu.PrefetchScalarGridSpec(
        num_scalar_prefetch=1,
        grid=(DB, n_steps),
        in_specs=[
            pl.BlockSpec((1, rows, 2 * ATT_HD), lambda b, j, pt: (b, 0, 0)),
            pl.BlockSpec((None, rows, PAGE_SIZE * H), lambda b, j, pt: (0, 0, 0)),
            pl.BlockSpec((1, rows, PAGE_SIZE * H), lambda b, j, pt: (jnp.where(j == n_steps - 1, 1, 0), 0, 0)),
            pl.BlockSpec((rows, S * H), lambda b, j, pt: (0, 0)),
            pl.BlockSpec((1, S, H, 2 * ATT_HD), lambda b, j, pt: (b, 0, 0, 0)),
            pl.BlockSpec((1, S, H, ATT_VD), lambda b, j, pt: (b, 0, 0, 0)),
            pl.BlockSpec((4, ATT_HD), lambda b, j, pt: (0, 0)),
            pl.BlockSpec((S * H, ATT_VD), lambda b, j, pt: (0, 0)),
        ] + [page_spec(g) for g in range(pages_per_step)] + [page_spec(g) for g in range(pages_per_step)],
        out_specs=pl.BlockSpec((1, S * H, ATT_VD), lambda b, j, pt: (b, 0, 0)),
        scratch_shapes=[pltpu.VMEM((rows, 1), F32), pltpu.VMEM((rows, 1), F32), pltpu.VMEM((rows, ATT_VD), F32)],
    )
    out = pl.pallas_call(
        kern,
        grid_spec=grid_spec,
        out_shape=jax.ShapeDtypeStruct((DB, S * H, ATT_VD), F32),
        compiler_params=_cparams("parallel", "arbitrary"),
        name="sample_attention",
    )(page_table, qrows, tiles, tiles, new_tile,
      k32.reshape(DB, S, H, 2 * ATT_HD), v32.reshape(DB, S, H, ATT_VD), lam_qk, g_tile,
      *([cache_k] * pages_per_step), *([cache_v] * pages_per_step))
    return out.reshape(DB, S, ATT_V)


def _head_ones(n, seg):
    r = lax.broadcasted_iota(jnp.int32, (n, n), 0) // seg
    c = lax.broadcasted_iota(jnp.int32, (n, n), 1) // seg
    return jnp.where(r == c, 1.0, 0.0).astype(BF16)


def _segsum(x, ones_bd):
    outs = []
    for g in range(x.shape[1] // V7X_LANES):
        outs.append(_split_dot(x[:, g * V7X_LANES:(g + 1) * V7X_LANES], ones_bd))
    return jnp.concatenate(outs, axis=1)


def _rwkv_prep_core(p, prev, mu_ref, w0_ref, w2_ref, a0_ref, a2_ref, g2_ref, kks_ref, ka_ref, rk_ref, outs):
    r_ref, k_ref, v_ref, lw_ref, kk_ref, b_ref, g_ref, bonus_ref = outs
    W = RWKV_W
    xs = p + (prev - p) * mu_ref[...]
    r = xs[:, 0:W]
    k = xs[:, W:2 * W]
    v = xs[:, 2 * W:3 * W]
    wa = xs[:, 3 * W:3 * W + DECAY_LORA + AAA_LORA]
    gd = xs[:, 3 * W + DECAY_LORA + AAA_LORA:RWKV_PROJ]
    logw = -DECAY_SCALE * jax.nn.sigmoid(w0_ref[...] + _dot(jnp.tanh(wa).astype(BF16), w2_ref[...]))
    a = jax.nn.sigmoid(a0_ref[...] + _dot(wa.astype(BF16), a2_ref[...]))
    g = _dot(jax.nn.sigmoid(gd).astype(BF16), g2_ref[...])
    ones_bd = _head_ones(V7X_LANES, RWKV_HD)
    kk = k * kks_ref[...]
    kk = kk / jnp.maximum(jnp.sqrt(_segsum(kk * kk, ones_bd)), 1e-12)
    k = k * (1.0 + (a - 1.0) * ka_ref[...])
    r_ref[...] = r
    k_ref[...] = k
    v_ref[...] = v
    lw_ref[...] = logw
    kk_ref[...] = kk
    b_ref[...] = kk * a
    g_ref[...] = g
    bonus_ref[...] = _segsum(r * k * rk_ref[...], ones_bd) * v


def _rwkv_prep_prompt_kernel(p_ref, tail_ref, shift_ref, *rest):
    params, outs = rest[:9], rest[9:]
    i = pl.program_id(1)
    p = p_ref[0]
    tm = p.shape[0]
    first = jnp.where(i == 0, shift_ref[0], tail_ref[0, V7X_SUBLANES - 1:V7X_SUBLANES, :])
    row = lax.broadcasted_iota(jnp.int32, p.shape, 0)
    prev = jnp.where(row == 0, first, pltpu.roll(p, 1, 0))
    _rwkv_prep_core(p, prev, *params, [o.at[0] for o in outs])


def _rwkv_prep_sample_kernel(p_ref, prev_ref, *rest):
    params, outs = rest[:9], rest[9:]
    _rwkv_prep_core(p_ref[...], prev_ref[...], *params, outs)


def _rwkv_param_list(shift_mu, rwkv_w0, rwkv_w2, rwkv_a0, rwkv_a2, rwkv_g2, rwkv_kk, rwkv_ka, rwkv_rk):
    zeros = jnp.zeros((AAA_LORA, RWKV_W), F32)
    w2p = jnp.concatenate([rwkv_w2, zeros], axis=0).astype(BF16)
    a2p = jnp.concatenate([jnp.zeros((DECAY_LORA, RWKV_W), F32), rwkv_a2], axis=0).astype(BF16)
    row = lambda t: t.reshape(1, -1)
    return [row(shift_mu), row(rwkv_w0), w2p, row(rwkv_a0), a2p, rwkv_g2.astype(BF16),
            row(rwkv_kk), row(rwkv_ka), row(rwkv_rk)]


def _rwkv_prep_prompt(p3, shift0, params, tm):
    B, T, _ = p3.shape
    const = lambda b, i: (0, 0)
    tail_blocks = tm // V7X_SUBLANES
    in_specs = [
        pl.BlockSpec((1, tm, RWKV_PROJ), lambda b, i: (b, i, 0)),
        pl.BlockSpec((1, V7X_SUBLANES, RWKV_PROJ), lambda b, i: (b, jnp.maximum(i * tail_blocks - 1, 0), 0)),
        pl.BlockSpec((1, 1, RWKV_PROJ), lambda b, i: (b, 0, 0)),
    ] + [pl.BlockSpec(t.shape, const) for t in params]
    out_shape = tuple(jax.ShapeDtypeStruct((B, T, RWKV_W), F32) for _ in range(8))
    out_specs = tuple(pl.BlockSpec((1, tm, RWKV_W), lambda b, i: (b, i, 0)) for _ in range(8))
    return pl.pallas_call(
        _rwkv_prep_prompt_kernel,
        grid=(B, T // tm),
        in_specs=in_specs, out_specs=out_specs, out_shape=out_shape,
        compiler_params=_cparams("parallel", "parallel"),
        name="rwkv_prep_prompt",
    )(p3, p3, shift0.reshape(B, 1, RWKV_PROJ), *params)


def _rwkv_prep_sample(p2, prev2, params):
    n = p2.shape[0]
    const = lambda i: (0, 0)
    in_specs = [pl.BlockSpec((n, RWKV_PROJ), const), pl.BlockSpec((n, RWKV_PROJ), const)] + [
        pl.BlockSpec(t.shape, const) for t in params]
    out_shape = tuple(jax.ShapeDtypeStruct((n, RWKV_W), F32) for _ in range(8))
    out_specs = tuple(pl.BlockSpec((n, RWKV_W), const) for _ in range(8))
    return pl.pallas_call(
        _rwkv_prep_sample_kernel,
        grid=(1,),
        in_specs=in_specs, out_specs=out_specs, out_shape=out_shape,
        compiler_params=_cparams("arbitrary"),
        name="rwkv_prep_sample",
    )(p2, prev2, *params)


def _rwkv_chunk_kernel(r_ref, k_ref, v_ref, lw_ref, kk_ref, b_ref, s0_ref, o_ref, sout_ref, s_scr, *, chunk):
    C = chunk
    c_idx = pl.program_id(1)
    n_pairs = RWKV_HEADS // 2
    L = V7X_LANES

    @pl.when(c_idx == 0)
    def _():
        s_scr[...] = s0_ref[0]

    ti = lax.broadcasted_iota(jnp.int32, (C, C), 0)
    si = lax.broadcasted_iota(jnp.int32, (C, C), 1)
    tri = jnp.where(si <= ti, 1.0, 0.0).astype(BF16)
    lw = lw_ref[0]
    hi = lw.astype(BF16)
    rem = lw - hi.astype(F32)
    mid = rem.astype(BF16)
    lo = (rem - mid.astype(F32)).astype(BF16)
    cs = _dot(tri, hi) + _dot(tri, mid) + _dot(tri, lo)
    g = jnp.exp(cs)
    gprev = jnp.exp(cs - lw)
    ginv = jnp.exp(-cs)
    kap_all = kk_ref[0] * gprev
    rt_all = r_ref[0] * g
    kt_all = k_ref[0] * ginv
    bt_all = b_ref[0] * ginv
    v_all = v_ref[0]
    g_last = g[C - 1:C, :]

    lane = lax.broadcasted_iota(jnp.int32, (C, L), 1)
    in_h0 = lane < RWKV_HD

    def stack(x):
        z = jnp.zeros_like(x)
        return jnp.concatenate([jnp.where(in_h0, x, z), jnp.where(in_h0, z, x)], axis=0)

    rr = lax.broadcasted_iota(jnp.int32, (2 * C, 2 * C), 0)
    cc = lax.broadcasted_iota(jnp.int32, (2 * C, 2 * C), 1)
    same = (rr // C) == (cc // C)
    strict = same & (cc < rr)
    incl = same & (cc <= rr)
    eye = jnp.where(rr == cc, 1.0, 0.0).astype(F32)
    n_sq = max(int(math.ceil(math.log2(C))) - 1, 0)

    for pr in range(n_pairs):
        sl = slice(pr * L, (pr + 1) * L)
        kap = stack(kap_all[:, sl]).astype(BF16)
        rt = stack(rt_all[:, sl]).astype(BF16)
        kt = stack(kt_all[:, sl]).astype(BF16)
        bt = stack(bt_all[:, sl]).astype(BF16)
        vs = stack(v_all[:, sl])
        vs16 = vs.astype(BF16)
        lhs = jnp.concatenate([kap, rt], axis=0)
        rhs = jnp.concatenate([kt, bt], axis=0)
        A = _dot_nt(lhs, rhs)
        zero = jnp.zeros((2 * C, 2 * C), F32)
        Lk = jnp.where(strict, A[0:2 * C, 0:2 * C], zero)
        Lb = jnp.where(strict, A[0:2 * C, 2 * C:4 * C], zero)
        Mk = jnp.where(incl, A[2 * C:4 * C, 0:2 * C], zero)
        Mb = jnp.where(incl, A[2 * C:4 * C, 2 * C:4 * C], zero)
        X = eye - Lb
        P16 = Lb.astype(BF16)
        for _ in range(n_sq):
            P = _dot(P16, P16)
            P16 = P.astype(BF16)
            X = X + _dot(X.astype(BF16), P16)
        S = s_scr[pr]
        KS = _dot_nt(lhs, S.astype(BF16))
        LkV = _dot(Lk.astype(BF16), vs16)
        U = _dot(X.astype(BF16), (KS[0:2 * C] + LkV).astype(BF16))
        U16 = U.astype(BF16)
        MkMb = jnp.concatenate([Mk, -Mb], axis=1).astype(BF16)
        O = KS[2 * C:4 * C] + _dot(MkMb, jnp.concatenate([vs16, U16], axis=0))
        o_ref[0, :, sl] = O[0:C] + O[C:2 * C]
        upd = _dot_tn(jnp.concatenate([vs16, -U16], axis=0), rhs)
        s_scr[pr] = (S + upd) * g_last[:, sl]

    @pl.when(c_idx == pl.num_programs(1) - 1)
    def _():
        sout_ref[0] = s_scr[...]


def _pair_state(S):
    B = S.shape[0]
    S = S.reshape(B, RWKV_HEADS // 2, 2, RWKV_HD, RWKV_HD)
    z = jnp.zeros_like(S[:, :, 0])
    top = jnp.concatenate([S[:, :, 0], z], axis=-1)
    bot = jnp.concatenate([z, S[:, :, 1]], axis=-1)
    return jnp.concatenate([top, bot], axis=-2)


def _unpair_state(Sbd):
    B = Sbd.shape[0]
    a = Sbd[:, :, :RWKV_HD, :RWKV_HD]
    b = Sbd[:, :, RWKV_HD:, RWKV_HD:]
    return jnp.stack([a, b], axis=2).reshape(B, RWKV_HEADS, RWKV_HD, RWKV_HD)


def _rwkv_chunked(r, k, v, lw, kk, b, S0, chunk):
    B, T, _ = r.shape
    n_pairs = RWKV_HEADS // 2
    tok = pl.BlockSpec((1, chunk, RWKV_W), lambda bb, c: (bb, c, 0))
    st = pl.BlockSpec((1, n_pairs, V7X_LANES, V7X_LANES), lambda bb, c: (bb, 0, 0, 0))
    o, s_out = pl.pallas_call(
        functools.partial(_rwkv_chunk_kernel, chunk=chunk),
        grid=(B, T // chunk),
        in_specs=[tok] * 6 + [st],
        out_specs=(tok, st),
        out_shape=(jax.ShapeDtypeStruct((B, T, RWKV_W), F32),
                   jax.ShapeDtypeStruct((B, n_pairs, V7X_LANES, V7X_LANES), F32)),
        scratch_shapes=[pltpu.VMEM((n_pairs, V7X_LANES, V7X_LANES), F32)],
        compiler_params=_cparams("parallel", "arbitrary"),
        name="rwkv_chunk_%d" % chunk,
    )(r, k, v, lw, kk, b, _pair_state(S0))
    return o, _unpair_state(s_out)


def _layer_norm(x, g, b):
    mu = jnp.mean(x, axis=-1, keepdims=True)
    var = jnp.mean(jnp.square(x - mu), axis=-1, keepdims=True)
    return (x - mu) * lax.rsqrt(var + LN_EPS) * g + b


def _mix_out_kernel(ga_ref, gr_ref, att_ref, o_ref, bonus_ref, g_ref, x_ref, wout_ref, gng_ref, gnb_ref,
                    ln_g_ref, ln_b_ref, h32_ref, h16_ref):
    ones_bd = _head_ones(V7X_LANES, RWKV_HD)
    o = o_ref[...]
    mean = _segsum(o, ones_bd) * (1.0 / RWKV_HD)
    d = o - mean
    var = _segsum(d * d, ones_bd) * (1.0 / RWKV_HD)
    rw = d * lax.rsqrt(var + RWKV_GN_EPS) * gng_ref[...] + gnb_ref[...]
    rw = (rw + bonus_ref[...]) * g_ref[...]
    mixed_in = jax.nn.sigmoid(ga_ref[...]) * att_ref[...] + jax.nn.sigmoid(gr_ref[...]) * rw
    mixed = _dot(mixed_in.astype(BF16), wout_ref[...])
    h = _layer_norm(DN_ALPHA * x_ref[...] + mixed, ln_g_ref[...], ln_b_ref[...])
    h32_ref[...] = h
    h16_ref[...] = h.astype(BF16)


def _mix_out(gates, att, o, bonus, g, x2d, wout16, gn_g, gn_b, ln_g, ln_b, tm):
    n = x2d.shape[0]
    row = lambda i: (i, 0)
    const = lambda i: (0, 0)
    tok = pl.BlockSpec((tm, D_MODEL), row)
    vec = pl.BlockSpec((1, D_MODEL), const)
    return pl.pallas_call(
        _mix_out_kernel,
        grid=(n // tm,),
        in_specs=[tok, pl.BlockSpec((tm, D_MODEL), lambda i: (i, 1)), tok, tok, tok, tok, tok,
                  pl.BlockSpec((D_MODEL, D_MODEL), const), vec, vec, vec, vec],
        out_specs=(tok, tok),
        out_shape=(jax.ShapeDtypeStruct((n, D_MODEL), F32), jax.ShapeDtypeStruct((n, D_MODEL), BF16)),
        compiler_params=_cparams("parallel"),
        name="mix_out",
    )(gates, gates, att, o, bonus, g, x2d, wout16, gn_g.reshape(1, -1), gn_b.reshape(1, -1),
      ln_g.reshape(1, -1), ln_b.reshape(1, -1))


_STAIR = tuple((p, q) for p in range(PEER_TOPK) for q in range(PEER_TOPK // (p + 1)))
_STAIR_ROWS = -(-len(_STAIR) // V7X_SUBLANES) * V7X_SUBLANES


def _top_ranks(x, k):
    R = x.shape[0]
    row = lax.broadcasted_iota(jnp.int32, x.shape, 0)
    rank = jnp.full(x.shape, float(k), F32)
    vals = []
    for it in range(k):
        m = jnp.max(x, axis=0, keepdims=True)
        idx = jnp.min(jnp.where(x == m, row, R), axis=0, keepdims=True)
        hit = row == idx
        rank = jnp.where(hit, float(it), rank)
        x = jnp.where(hit, -jnp.inf, x)
        vals.append(m)
    return vals, rank


def _peer_select_kernel(h_ref, wq_ref, keys_ref, cut_ref, f1_ref, rank2_ref, e2_ref, cand_scr, sel_scr):
    q = _dot(h_ref[...], wq_ref[...]).astype(BF16)
    K = PEER_TOPK
    for h in range(PEER_HEADS):
        s = []
        for c in range(2):
            blk = (h * 2 + c) * PEER_HALF
            s.append(_dot_nt(keys_ref[h * 2 + c], q[:, blk:blk + PEER_HALF]))
        a1, rank1 = _top_ranks(s[0], K)
        a2, rank2 = _top_ranks(s[1], K)
        cand_scr[...] = jnp.full(cand_scr.shape, -jnp.inf, F32)
        for i, (p, qq) in enumerate(_STAIR):
            cand_scr[i:i + 1, :] = a1[p] + a2[qq]
        cand = cand_scr[...]
        _, crank = _top_ranks(cand, K)
        chosen = crank < float(K)
        z = jnp.sum(jnp.where(chosen, jnp.exp(cand - (a1[0] + a2[0])), 0.0), axis=0, keepdims=True)
        sel_scr[...] = jnp.where(chosen, 1.0, 0.0)
        cut = jnp.zeros(rank1.shape, F32)
        start = 0
        for p in range(K):
            n_p = K // (p + 1)
            cut_p = jnp.sum(sel_scr[start:start + n_p, :], axis=0, keepdims=True)
            cut = jnp.where(rank1 == float(p), cut_p, cut)
            start += n_p
        cut_ref[h] = cut
        f1_ref[h] = jnp.exp(s[0] - a1[0]) / z
        rank2_ref[h] = rank2
        e2_ref[h] = jnp.exp(s[1] - a2[0])


def _peer_select(h16, wq16, keys16, tp):
    n = h16.shape[0]
    sel_shape = jax.ShapeDtypeStruct((PEER_HEADS, N_KEYS, n), F32)
    sel_spec = pl.BlockSpec((PEER_HEADS, N_KEYS, tp), lambda i: (0, 0, i))
    return pl.pallas_call(
        _peer_select_kernel,
        grid=(n // tp,),
        in_specs=[pl.BlockSpec((tp, D_MODEL), lambda i: (i, 0)),
                  pl.BlockSpec(wq16.shape, lambda i: (0, 0)),
                  pl.BlockSpec(keys16.shape, lambda i: (0, 0, 0))],
        out_specs=(sel_spec,) * 4,
        out_shape=(sel_shape,) * 4,
        scratch_shapes=[pltpu.VMEM((_STAIR_ROWS, tp), F32), pltpu.VMEM((_STAIR_ROWS, tp), F32)],
        compiler_params=_cparams("parallel"),
        name="peer_select",
    )(h16, wq16, keys16)


_PEER_I1_PER_BLOCK = 8
_SQRT_HALF = math.sqrt(0.5)


def _peer_dense_kernel(x_ref, u_ref, vt_ref, cut_ref, f1_ref, rank2_ref, e2_ref, h_ref, ln_g_ref, ln_b_ref,
                       y_ref, acc_scr, a_scr):
    e = pl.program_id(1)

    @pl.when(e == 0)
    def _():
        acc_scr[...] = jnp.zeros(acc_scr.shape, F32)

    hid_t = _dot_nt(u_ref[...], x_ref[...])
    for il in range(_PEER_I1_PER_BLOCK):
        w = jnp.zeros((N_KEYS, x_ref.shape[0]), F32)
        for h in range(PEER_HEADS):
            cutrow = cut_ref[h, il:il + 1, :]
            f1row = f1_ref[h, il:il + 1, :]
            w = w + jnp.where(rank2_ref[h] < cutrow, e2_ref[h] * f1row, 0.0)
        hb = hid_t[il * N_KEYS:(il + 1) * N_KEYS]
        act = 0.5 * hb * (1.0 + lax.erf(hb * _SQRT_HALF))
        a_scr[il * N_KEYS:(il + 1) * N_KEYS, :] = (w * act).astype(BF16)
    acc_scr[...] += _dot(vt_ref[...], a_scr[...])

    @pl.when(e == pl.num_programs(1) - 1)
    def _():
        ff = acc_scr[...].T
        y_ref[...] = _layer_norm(DN_ALPHA * h_ref[...] + ff, ln_g_ref[...], ln_b_ref[...])


def _peer_dense(h32, h16, u16, vt16, sel, ln_g, ln_b, tt):
    n = h32.shape[0]
    cut, f1, rank2, e2 = sel
    eb = _PEER_I1_PER_BLOCK * N_KEYS
    tok = lambda i, e: (i, 0)
    const = lambda i, e: (0, 0)
    return pl.pallas_call(
        _peer_dense_kernel,
        grid=(n // tt, N_EXPERTS // eb),
        in_specs=[pl.BlockSpec((tt, D_MODEL), tok),
                  pl.BlockSpec((eb, D_MODEL), lambda i, e: (e, 0)),
                  pl.BlockSpec((D_MODEL, eb), lambda i, e: (0, e)),
                  pl.BlockSpec((PEER_HEADS, _PEER_I1_PER_BLOCK, tt), lambda i, e: (0, e, i)),
                  pl.BlockSpec((PEER_HEADS, _PEER_I1_PER_BLOCK, tt), lambda i, e: (0, e, i)),
                  pl.BlockSpec((PEER_HEADS, N_KEYS, tt), lambda i, e: (0, 0, i)),
                  pl.BlockSpec((PEER_HEADS, N_KEYS, tt), lambda i, e: (0, 0, i)),
                  pl.BlockSpec((tt, D_MODEL), tok),
                  pl.BlockSpec((1, D_MODEL), const), pl.BlockSpec((1, D_MODEL), const)],
        out_specs=pl.BlockSpec((tt, D_MODEL), tok),
        out_shape=jax.ShapeDtypeStruct((n, D_MODEL), F32),
        scratch_shapes=[pltpu.VMEM((D_MODEL, tt), F32), pltpu.VMEM((eb, tt), BF16)],
        compiler_params=_cparams("parallel", "arbitrary"),
        name="peer_dense",
    )(h16, u16, vt16, cut, f1, rank2, e2, h32, ln_g.reshape(1, -1), ln_b.reshape(1, -1))


_TM_PROMPT = 256
_TQ_PROMPT = 256
_RWKV_CHUNK = 64
_RWKV_CHUNK_SAMPLE = 16
_PAGES_PER_STEP = 4
_PEER_TP = 256
_PEER_TT = 512


def _layer_tail(gates, att2, o2, bonus2, g2, x2, lw, tm, tp, tt):
    h32, h16 = _mix_out(gates, att2, o2, bonus2, g2, x2, lw["wout16"], lw["rwkv_ln_g"], lw["rwkv_ln_b"],
                        lw["ln1_g"], lw["ln1_b"], tm)
    sel = _peer_select(h16, lw["wq16"], lw["keys16"], tp)
    return _peer_dense(h32, h16, lw["u16"], lw["vt16"], sel, lw["ln2_g"], lw["ln2_b"], tt)


def _prompt_layer(x, lw, rel_bias):
    B, T, _ = x.shape
    n = B * T
    x2 = x.reshape(n, D_MODEL)
    q16, k32, k16, v32, v16, gates, p = _in_proj(x2, lw["w_in16"], _TM_PROMPT)
    sh = lambda t: t.reshape(B, T, -1)
    att = _prompt_attention(sh(q16), sh(k16), sh(v16), rel_bias, lw["lam_qk"], lw["attn_norm_g"], _TQ_PROMPT)
    p3 = sh(p)
    shift0 = jnp.zeros((B, RWKV_PROJ), F32)
    r, k, v, lgw, kk, b, g, bonus = _rwkv_prep_prompt(p3, shift0, lw["rwkv_params"], _TM_PROMPT)
    S0 = jnp.zeros((B, RWKV_HEADS, RWKV_HD, RWKV_HD), F32)
    o, S_new = _rwkv_chunked(r, k, v, lgw, kk, b, S0, _RWKV_CHUNK)
    fl = lambda t: t.reshape(n, -1)
    y = _layer_tail(gates, fl(att), fl(o), fl(bonus), fl(g), x2, lw, _TM_PROMPT, _PEER_TP, _PEER_TT)
    return (y.reshape(B, T, D_MODEL), k32.reshape(B, T, ATT_HEADS, 2 * ATT_HD), v32.reshape(B, T, ATT_HEADS, ATT_VD),
            S_new, p3[:, -1, :])


def _sample_layer(x, lw, rel_bias, cache_k, cache_v, layer, page_table, shift0, S0):
    DB, S, _ = x.shape
    n = DB * S
    x2 = x.reshape(n, D_MODEL)
    q16, k32, k16, v32, v16, gates, p = _in_proj(x2, lw["w_in16"], n)
    sh = lambda t: t.reshape(DB, S, -1)
    att = _sample_attention(sh(q16), sh(k32), sh(v32), cache_k, cache_v, layer, page_table, rel_bias,
                            lw["lam_qk"], lw["attn_norm_g"], _PAGES_PER_STEP)
    p3 = sh(p)
    prev = jnp.concatenate([shift0[:, None, :], p3[:, :-1, :]], axis=1).reshape(n, RWKV_PROJ)
    r, k, v, lgw, kk, b, g, bonus = _rwkv_prep_sample(p, prev, lw["rwkv_params"])
    pad = lambda t: jnp.pad(sh(t), ((0, 0), (0, _RWKV_CHUNK_SAMPLE - S), (0, 0)))
    o, S_new = _rwkv_chunked(pad(r), pad(k), pad(v), pad(lgw), pad(kk), pad(b), S0, _RWKV_CHUNK_SAMPLE)
    o2 = o[:, :S, :].reshape(n, RWKV_W)
    y = _layer_tail(gates, att.reshape(n, ATT_V), o2, bonus, g, x2, lw, n, n, n)
    return (y.reshape(DB, S, D_MODEL), k32.reshape(DB, S, ATT_HEADS, 2 * ATT_HD), v32.reshape(DB, S, ATT_HEADS, ATT_VD),
            S_new, p3[:, -1, :])


def kernel(x_prompt, x_sample, cache_k, cache_v, state_wkv, state_shift, page_table, w_in, w_out, lam_qk, attn_norm_g, shift_mu, rwkv_w0, rwkv_w2, rwkv_a0, rwkv_a2, rwkv_g2, rwkv_kk, rwkv_ka, rwkv_rk, rwkv_ln_g, rwkv_ln_b, ln1_g, ln1_b, ln2_g, ln2_b, peer_wq, peer_keys, peer_u, peer_v, rel_bias):
    assert w_in.shape[0] == DEPTH == 1
    yp, ys = x_prompt, x_sample
    outs = [[] for _ in range(8)]
    for l in range(DEPTH):
        lw = {
            "w_in16": w_in[l].astype(BF16), "wout16": w_out[l].astype(BF16),
            "lam_qk": lam_qk[l], "attn_norm_g": attn_norm_g[l],
            "rwkv_params": _rwkv_param_list(shift_mu[l], rwkv_w0[l], rwkv_w2[l], rwkv_a0[l], rwkv_a2[l],
                                            rwkv_g2[l], rwkv_kk[l], rwkv_ka[l], rwkv_rk[l]),
            "rwkv_ln_g": rwkv_ln_g[l], "rwkv_ln_b": rwkv_ln_b[l],
            "ln1_g": ln1_g[l], "ln1_b": ln1_b[l], "ln2_g": ln2_g[l], "ln2_b": ln2_b[l],
            "wq16": peer_wq[l].astype(BF16),
            "keys16": peer_keys[l].reshape(PEER_HEADS * 2, N_KEYS, PEER_HALF).astype(BF16),
            "u16": peer_u[l].astype(BF16), "vt16": peer_v[l].T.astype(BF16),
        }
        yp, kp, vp, Sp, shp = _prompt_layer(yp, lw, rel_bias)
        ys, ksm, vsm, Ss, shs = _sample_layer(ys, lw, rel_bias, cache_k, cache_v, l, page_table,
                                              state_shift[l], state_wkv[l])
        for lst, val in zip(outs, (kp, vp, ksm, vsm, Sp, Ss, shp, shs)):
            lst.append(val)
    stacked = tuple(jnp.stack(lst, axis=0) for lst in outs)
    return (yp, ys) + stacked
```

```python
import functools
import math

import jax
import jax.numpy as jnp
from jax import lax
from jax.experimental import pallas as pl
from jax.experimental.pallas import tpu as pltpu

D_MODEL = 1024
PAGE_SIZE = 128
ATT_HD = 64
ATT_HEADS = D_MODEL // (2 * ATT_HD)
ATT_VD = 2 * ATT_HD
ATT_QK = ATT_HEADS * 2 * ATT_HD
ATT_V = ATT_HEADS * ATT_VD
N_BUCKETS = 32
MAX_DISTANCE = 128
ATT_EPS = 1e-5
RWKV_HD = 64
RWKV_HEADS = D_MODEL // RWKV_HD
RWKV_W = RWKV_HEADS * RWKV_HD
DECAY_LORA = 64
AAA_LORA = 64
GATE_LORA = 128
RWKV_PROJ = 3 * RWKV_W + DECAY_LORA + AAA_LORA + GATE_LORA
DECAY_SCALE = 0.606531
RWKV_GN_EPS = 64e-5
O_K = ATT_QK
O_V = O_K + ATT_QK
O_GA = O_V + ATT_V
O_GR = O_GA + D_MODEL
O_RW = O_GR + D_MODEL
N_IN = O_RW + RWKV_PROJ
PEER_HEADS = 8
N_KEYS = 128
N_EXPERTS = N_KEYS * N_KEYS
PEER_QDIM = 256
PEER_HALF = PEER_QDIM // 2
PEER_TOPK = 16
DEPTH = 1
DN_ALPHA = (2.0 * DEPTH) ** 0.25
LN_EPS = 1e-5
LAMBDA_INIT = 0.8 - 0.6 * math.exp(-0.3 * 0)

V7X_LANES = 128
V7X_SUBLANES = 8
V7X_VMEM_LIMIT_BYTES = 56 * 1024 * 1024

NEG_INF = -1e30
LOG2E = math.log2(math.e)
BF16 = jnp.bfloat16
F32 = jnp.float32

NT_DIMS = (((1,), (1,)), ((), ()))
TN_DIMS = (((0,), (0,)), ((), ()))


def _cparams(*sem, flags=None):
    return pltpu.CompilerParams(dimension_semantics=sem, vmem_limit_bytes=V7X_VMEM_LIMIT_BYTES, flags=flags)


def _dot(a, b):
    return jnp.dot(a, b, preferred_element_type=F32)


def _dot_nt(a, b):
    return lax.dot_general(a, b, NT_DIMS, preferred_element_type=F32)


def _dot_tn(a, b):
    return lax.dot_general(a, b, TN_DIMS, preferred_element_type=F32)


def _split_dot(x, w_bf16):
    hi = x.astype(BF16)
    lo = (x - hi.astype(F32)).astype(BF16)
    return _dot(hi, w_bf16) + _dot(lo, w_bf16)


_P_CHUNKS = ((0, 1024), (1024, 2048), (2048, 3072), (3072, RWKV_PROJ))


def _in_proj_kernel(x_ref, w_ref, q16_ref, k32_ref, k16_ref, v32_ref, v16_ref, gates_ref, p_ref):
    x = x_ref[...].astype(BF16)

    def mm(lo, hi):
        return _dot(x, w_ref[:, lo:hi])

    q16_ref[...] = (mm(0, O_K) * (ATT_HD ** -0.5 * LOG2E)).astype(BF16)
    k = mm(O_K, O_V)
    k32_ref[...] = k
    k16_ref[...] = k.astype(BF16)
    v = mm(O_V, O_GA)
    v32_ref[...] = v
    v16_ref[...] = v.astype(BF16)
    gates_ref[:, 0:D_MODEL] = mm(O_GA, O_GR)
    gates_ref[:, D_MODEL:2 * D_MODEL] = mm(O_GR, O_RW)
    for lo, hi in _P_CHUNKS:
        p_ref[:, lo:hi] = mm(O_RW + lo, O_RW + hi)


def _in_proj(x2d, w_bf16, tm):
    n = x2d.shape[0]
    row = lambda i: (i, 0)
    full = lambda i: (0, 0)
    out_shape = (
        jax.ShapeDtypeStruct((n, ATT_QK), BF16),
        jax.ShapeDtypeStruct((n, ATT_QK), F32),
        jax.ShapeDtypeStruct((n, ATT_QK), BF16),
        jax.ShapeDtypeStruct((n, ATT_V), F32),
        jax.ShapeDtypeStruct((n, ATT_V), BF16),
        jax.ShapeDtypeStruct((n, 2 * D_MODEL), F32),
        jax.ShapeDtypeStruct((n, RWKV_PROJ), F32),
    )
    out_specs = tuple(pl.BlockSpec((tm, s.shape[1]), row) for s in out_shape)
    return pl.pallas_call(
        _in_proj_kernel,
        grid=(n // tm,),
        in_specs=[pl.BlockSpec((tm, D_MODEL), row),
                  pl.BlockSpec((D_MODEL, N_IN), full, pipeline_mode=pl.Buffered(1))],
        out_specs=out_specs,
        out_shape=out_shape,
        compiler_params=_cparams("parallel"),
        name="in_proj",
    )(x2d, w_bf16)


def _t5_bucket(rel):
    n = jnp.maximum(rel, 0)
    max_exact = N_BUCKETS // 2
    nf = jnp.maximum(n, 1).astype(F32)
    large = max_exact + (jnp.log(nf / max_exact) / math.log(MAX_DISTANCE / max_exact)
                         * (N_BUCKETS - max_exact)).astype(jnp.int32)
    large = jnp.minimum(large, N_BUCKETS - 1)
    return jnp.where(n < max_exact, n, large)


def _shifted_bias(rel_bias, dist):
    far = lax.dynamic_index_in_dim(rel_bias, _t5_bucket(jnp.int32(2 * MAX_DISTANCE)), 0, keepdims=False)
    table = (rel_bias - far[None, :]) * LOG2E
    bucket = _t5_bucket(dist)[None]
    expand = (slice(None),) + (None,) * dist.ndim
    out = jnp.zeros((rel_bias.shape[1],) + dist.shape, F32)
    for b in range(N_BUCKETS):
        out = jnp.where(bucket == b, table[b][expand], out)
    return out


def _per_row_head(tiles, row_h):
    out = jnp.zeros(tiles.shape[1:], F32)
    for h in range(tiles.shape[0]):
        out = jnp.where(row_h[:, None] == h, tiles[h], out)
    return out


def _lambda_value(lq):
    s01 = jnp.sum(lq[0:1] * lq[1:2], axis=-1, keepdims=True)
    s23 = jnp.sum(lq[2:3] * lq[3:4], axis=-1, keepdims=True)
    return jnp.exp(s01) - jnp.exp(s23) + LAMBDA_INIT


def _rms_scale(o):
    return o * lax.rsqrt(jnp.mean(o * o, axis=-1, keepdims=True) + ATT_EPS)


def _prompt_attn_kernel(q_ref, k_ref, vt_ref, bias_ref, lam_ref, g_ref, o_ref, m_ref, l_ref, acc_ref, *, tq):
    i = pl.program_id(2)
    lane = lax.broadcasted_iota(jnp.int32, (tq, 2 * ATT_HD), 1)
    q = q_ref[0]
    zero = jnp.zeros_like(q)
    qm = (jnp.where(lane < ATT_HD, q, zero), jnp.where(lane >= ATT_HD, q, zero))

    m_ref[...] = jnp.full(m_ref.shape, NEG_INF, F32)
    l_ref[...] = jnp.zeros(l_ref.shape, F32)
    acc_ref[...] = jnp.zeros(acc_ref.shape, F32)

    def step(j, add_t):
        start = pl.multiple_of(j * tq, tq)
        kb = k_ref[0, pl.ds(start, tq), :]
        vt = vt_ref[0, :, pl.ds(start, tq)]
        s = [_dot_nt(kb, qm[c]) for c in range(2)]
        p16, alphas = [], []
        for c in range(2):
            sc = s[c] if add_t is None else s[c] + add_t
            m_old = m_ref[c]
            m_new = jnp.maximum(m_old, jnp.max(sc, axis=0, keepdims=True))
            alpha = jnp.exp2(m_old - m_new)
            p = jnp.exp2(sc - m_new)
            l_ref[c] = alpha * l_ref[c] + jnp.sum(p, axis=0, keepdims=True)
            m_ref[c] = m_new
            p16.append(p.astype(BF16))
            alphas.append(alpha)
        for c in range(2):
            acc_ref[c] = alphas[c] * acc_ref[c] + _dot(vt, p16[c])

    def far_body(j, carry):
        step(j, None)
        return carry

    lax.fori_loop(0, jnp.maximum(i - 1, 0), far_body, 0)

    @pl.when(i > 0)
    def _():
        step(i - 1, bias_ref[0, 1])

    step(i, bias_ref[0, 0])

    lam = _lambda_value(lam_ref[...])
    ot = acc_ref[0] / l_ref[0] - lam * (acc_ref[1] / l_ref[1])
    ot = ot * lax.rsqrt(jnp.mean(ot * ot, axis=0, keepdims=True) + ATT_EPS)
    o_ref[0] = ot.T * (g_ref[...] * (1.0 - LAMBDA_INIT))


def _prompt_attention(q16, k16, vt16, rel_bias, lam_qk, attn_norm_g, tq):
    B, T, _ = q16.shape
    r = jnp.arange(tq, dtype=jnp.int32)
    diff_t = r[None, :] - r[:, None]
    diag = jnp.where(diff_t >= 0, _shifted_bias(rel_bias, diff_t), NEG_INF)
    sub = _shifted_bias(rel_bias, diff_t + tq)
    bias = jnp.stack([diag, sub], axis=1).astype(F32)
    kern = functools.partial(_prompt_attn_kernel, tq=tq)
    return pl.pallas_call(
        kern,
        grid=(B, ATT_HEADS, T // tq),
        in_specs=[
            pl.BlockSpec((1, tq, 2 * ATT_HD), lambda b, h, i: (b, i, h)),
            pl.BlockSpec((1, T, 2 * ATT_HD), lambda b, h, i: (b, 0, h)),
            pl.BlockSpec((1, ATT_VD, T), lambda b, h, i: (b, h, 0)),
            pl.BlockSpec((1, 2, tq, tq), lambda b, h, i: (h, 0, 0, 0)),
            pl.BlockSpec((4, ATT_HD), lambda b, h, i: (0, 0)),
            pl.BlockSpec((1, ATT_VD), lambda b, h, i: (0, h)),
        ],
        out_specs=pl.BlockSpec((1, tq, ATT_VD), lambda b, h, i: (b, i, h)),
        out_shape=jax.ShapeDtypeStruct((B, T, ATT_V), F32),
        scratch_shapes=[pltpu.VMEM((2, 1, tq), F32), pltpu.VMEM((2, 1, tq), F32),
                        pltpu.VMEM((2, ATT_VD, tq), F32)],
        compiler_params=_cparams("parallel", "parallel", "arbitrary"),
        name="prompt_attention",
    )(q16, k16, vt16, bias, lam_qk, attn_norm_g.reshape(1, ATT_V))


def _sample_attn_kernel(pt_ref, q_ref, plain_ref, last_ref, new_ref, knew_ref, vnew_ref, lam_ref, g_ref,
                        *rest, pages_per_step):
    k_refs = rest[:pages_per_step]
    v_refs = rest[pages_per_step:2 * pages_per_step]
    o_ref, m_ref, l_ref, acc_ref = rest[2 * pages_per_step:]
    j = pl.program_id(1)
    last = pl.num_programs(1) - 1
    q = q_ref[0]

    @pl.when(j == 0)
    def _():
        m_ref[...] = jnp.full(m_ref.shape, NEG_INF, F32)
        l_ref[...] = jnp.zeros(l_ref.shape, F32)
        acc_ref[...] = jnp.zeros(acc_ref.shape, F32)

    def update(s, v2d):
        m_old = m_ref[...]
        m_new = jnp.maximum(m_old, jnp.max(s, axis=-1, keepdims=True))
        alpha = jnp.exp2(m_old - m_new)
        p = jnp.exp2(s - m_new)
        l_ref[...] = alpha * l_ref[...] + jnp.sum(p, axis=-1, keepdims=True)
        acc_ref[...] = alpha * acc_ref[...] + _dot(p.astype(BF16), v2d)
        m_ref[...] = m_new

    rows2d = PAGE_SIZE * ATT_HEADS
    k_all = jnp.concatenate([r[...].reshape(rows2d, 2 * ATT_HD).astype(BF16) for r in k_refs], axis=0)
    v_all = jnp.concatenate([r[...].reshape(rows2d, ATT_VD).astype(BF16) for r in v_refs], axis=0)
    add = jnp.concatenate([plain_ref[...]] * (pages_per_step - 1) + [last_ref[0]], axis=1)
    update(_dot_nt(q, k_all) + add, v_all)

    @pl.when(j == last)
    def _():
        n_new = knew_ref.shape[1]
        k2d = knew_ref[0].reshape(n_new * ATT_HEADS, 2 * ATT_HD).astype(BF16)
        v2d = vnew_ref[0].reshape(n_new * ATT_HEADS, ATT_VD).astype(BF16)
        update(_dot_nt(q, k2d) + new_ref[...], v2d)
        half = acc_ref.shape[0] // 2
        o = acc_ref[...] / l_ref[...]
        lam = _lambda_value(lam_ref[...])
        diff = o[0:half] - lam * o[half:2 * half]
        o_ref[0] = _rms_scale(diff) * (g_ref[...] * (1.0 - LAMBDA_INIT))


def _sample_attention(q16, k32, v32, cache_k, cache_v, layer, page_table, rel_bias, lam_qk, attn_norm_g,
                      pages_per_step):
    DB, S, _ = q16.shape
    n_pages = page_table.shape[1]
    past = n_pages * PAGE_SIZE
    H = ATT_HEADS
    rows = 2 * S * H
    q5 = q16.reshape(DB, S, H, 2, ATT_HD)
    qrows = jnp.stack([jnp.concatenate([q5[:, :, :, 0], jnp.zeros_like(q5[:, :, :, 0])], axis=-1),
                       jnp.concatenate([jnp.zeros_like(q5[:, :, :, 1]), q5[:, :, :, 1]], axis=-1)], axis=1)
    qrows = qrows.reshape(DB, rows, 2 * ATT_HD)
    row_s = (jnp.arange(rows, dtype=jnp.int32) // H) % S
    row_h = jnp.arange(rows, dtype=jnp.int32) % H
    col_t = jnp.arange(PAGE_SIZE * H, dtype=jnp.int32) // H
    col_h = jnp.arange(PAGE_SIZE * H, dtype=jnp.int32) % H
    same_head = row_h[:, None] == col_h[None, :]
    plain = jnp.where(same_head, 0.0, NEG_INF).astype(F32)
    dist_last = PAGE_SIZE + row_s[:, None] - col_t[None, :]
    b_last = _per_row_head(_shifted_bias(rel_bias, dist_last), row_h)
    last_tile = jnp.where(same_head, b_last, NEG_INF).astype(F32)
    tiles = jnp.stack([plain, last_tile], axis=0)
    ncol_t = jnp.arange(S * H, dtype=jnp.int32) // H
    ncol_h = jnp.arange(S * H, dtype=jnp.int32) % H
    dist_new = row_s[:, None] - ncol_t[None, :]
    b_new = _per_row_head(_shifted_bias(rel_bias, dist_new), row_h)
    ok_new = (row_h[:, None] == ncol_h[None, :]) & (dist_new >= 0)
    new_tile = jnp.where(ok_new, b_new, NEG_INF).astype(F32)
    g_tile = jnp.tile(attn_norm_g.reshape(H, ATT_VD), (S, 1))

    n_steps = n_pages // pages_per_step
    kern = functools.partial(_sample_attn_kernel, pages_per_step=pages_per_step)

    def page_spec(g):
        return pl.BlockSpec((None, None, PAGE_SIZE, H, 2 * ATT_HD),
                            lambda b, j, pt, g=g: (layer, pt[b, j * pages_per_step + g], 0, 0, 0))

    grid_spec = pltpu.PrefetchScalarGridSpec(
        num_scalar_prefetch=1,
        grid=(DB, n_steps),
        in_specs=[
            pl.BlockSpec((1, rows, 2 * ATT_HD), lambda b, j, pt: (b, 0, 0)),
            pl.BlockSpec((None, rows, PAGE_SIZE * H), lambda b, j, pt: (0, 0, 0)),
            pl.BlockSpec((1, rows, PAGE_SIZE * H), lambda b, j, pt: (jnp.where(j == n_steps - 1, 1, 0), 0, 0)),
            pl.BlockSpec((rows, S * H), lambda b, j, pt: (0, 0)),
            pl.BlockSpec((1, S, H, 2 * ATT_HD), lambda b, j, pt: (b, 0, 0, 0)),
            pl.BlockSpec((1, S, H, ATT_VD), lambda b, j, pt: (b, 0, 0, 0)),
            pl.BlockSpec((4, ATT_HD), lambda b, j, pt: (0, 0)),
            pl.BlockSpec((S * H, ATT_VD), lambda b, j, pt: (0, 0)),
        ] + [page_spec(g) for g in range(pages_per_step)] + [page_spec(g) for g in range(pages_per_step)],
        out_specs=pl.BlockSpec((1, S * H, ATT_VD), lambda b, j, pt: (b, 0, 0)),
        scratch_shapes=[pltpu.VMEM((rows, 1), F32), pltpu.VMEM((rows, 1), F32), pltpu.VMEM((rows, ATT_VD), F32)],
    )
    out = pl.pallas_call(
        kern,
        grid_spec=grid_spec,
        out_shape=jax.ShapeDtypeStruct((DB, S * H, ATT_VD), F32),
        compiler_params=_cparams("parallel", "arbitrary"),
        name="sample_attention",
    )(page_table, qrows, tiles, tiles, new_tile,
      k32.reshape(DB, S, H, 2 * ATT_HD), v32.reshape(DB, S, H, ATT_VD), lam_qk, g_tile,
      *([cache_k] * pages_per_step), *([cache_v] * pages_per_step))
    return out.reshape(DB, S, ATT_V)


def _head_ones(n, seg):
    r = lax.broadcasted_iota(jnp.int32, (n, n), 0) // seg
    c = lax.broadcasted_iota(jnp.int32, (n, n), 1) // seg
    return jnp.where(r == c, 1.0, 0.0).astype(BF16)


def _segsum(x, ones_bd):
    outs = []
    for g in range(x.shape[1] // V7X_LANES):
        outs.append(_split_dot(x[:, g * V7X_LANES:(g + 1) * V7X_LANES], ones_bd))
    return jnp.concatenate(outs, axis=1)


def _rwkv_prep_core(p, prev, mu_ref, w0_ref, w2_ref, a0_ref, a2_ref, g2_ref, kks_ref, ka_ref, rk_ref, outs):
    r_ref, k_ref, v_ref, lw_ref, kk_ref, b_ref, g_ref, bonus_ref = outs
    W = RWKV_W
    xs = p + (prev - p) * mu_ref[...]
    r = xs[:, 0:W]
    k = xs[:, W:2 * W]
    v = xs[:, 2 * W:3 * W]
    wa = xs[:, 3 * W:3 * W + DECAY_LORA + AAA_LORA]
    gd = xs[:, 3 * W + DECAY_LORA + AAA_LORA:RWKV_PROJ]
    logw = -DECAY_SCALE * jax.nn.sigmoid(w0_ref[...] + _dot(jnp.tanh(wa).astype(BF16), w2_ref[...]))
    a = jax.nn.sigmoid(a0_ref[...] + _dot(wa.astype(BF16), a2_ref[...]))
    g = _dot(jax.nn.sigmoid(gd).astype(BF16), g2_ref[...])
    ones_bd = _head_ones(V7X_LANES, RWKV_HD)
    kk = k * kks_ref[...]
    kk = kk / jnp.maximum(jnp.sqrt(_segsum(kk * kk, ones_bd)), 1e-12)
    k = k * (1.0 + (a - 1.0) * ka_ref[...])
    r_ref[...] = r
    k_ref[...] = k
    v_ref[...] = v
    lw_ref[...] = logw
    kk_ref[...] = kk
    b_ref[...] = kk * a
    g_ref[...] = g
    bonus_ref[...] = _segsum(r * k * rk_ref[...], ones_bd) * v


def _rwkv_prep_prompt_kernel(p_ref, tail_ref, shift_ref, *rest):
    params, outs = rest[:9], rest[9:]
    i = pl.program_id(1)
    p = p_ref[0]
    tm = p.shape[0]
    first = jnp.where(i == 0, shift_ref[0], tail_ref[0, V7X_SUBLANES - 1:V7X_SUBLANES, :])
    row = lax.broadcasted_iota(jnp.int32, p.shape, 0)
    prev = jnp.where(row == 0, first, pltpu.roll(p, 1, 0))
    _rwkv_prep_core(p, prev, *params, [o.at[0] for o in outs])


def _rwkv_prep_sample_kernel(p_ref, prev_ref, *rest):
    params, outs = rest[:9], rest[9:]
    _rwkv_prep_core(p_ref[...], prev_ref[...], *params, outs)


def _rwkv_param_list(shift_mu, rwkv_w0, rwkv_w2, rwkv_a0, rwkv_a2, rwkv_g2, rwkv_kk, rwkv_ka, rwkv_rk):
    zeros = jnp.zeros((AAA_LORA, RWKV_W), F32)
    w2p = jnp.concatenate([rwkv_w2, zeros], axis=0).astype(BF16)
    a2p = jnp.concatenate([jnp.zeros((DECAY_LORA, RWKV_W), F32), rwkv_a2], axis=0).astype(BF16)
    row = lambda t: t.reshape(1, -1)
    return [row(shift_mu), row(rwkv_w0), w2p, row(rwkv_a0), a2p, rwkv_g2.astype(BF16),
            row(rwkv_kk), row(rwkv_ka), row(rwkv_rk)]


def _rwkv_prep_prompt(p3, shift0, params, tm):
    B, T, _ = p3.shape
    const = lambda b, i: (0, 0)
    tail_blocks = tm // V7X_SUBLANES
    in_specs = [
        pl.BlockSpec((1, tm, RWKV_PROJ), lambda b, i: (b, i, 0)),
        pl.BlockSpec((1, V7X_SUBLANES, RWKV_PROJ), lambda b, i: (b, jnp.maximum(i * tail_blocks - 1, 0), 0)),
        pl.BlockSpec((1, 1, RWKV_PROJ), lambda b, i: (b, 0, 0)),
    ] + [pl.BlockSpec(t.shape, const) for t in params]
    out_shape = tuple(jax.ShapeDtypeStruct((B, T, RWKV_W), F32) for _ in range(8))
    out_specs = tuple(pl.BlockSpec((1, tm, RWKV_W), lambda b, i: (b, i, 0)) for _ in range(8))
    return pl.pallas_call(
        _rwkv_prep_prompt_kernel,
        grid=(B, T // tm),
        in_specs=in_specs, out_specs=out_specs, out_shape=out_shape,
        compiler_params=_cparams("parallel", "parallel"),
        name="rwkv_prep_prompt",
    )(p3, p3, shift0.reshape(B, 1, RWKV_PROJ), *params)


def _rwkv_prep_sample(p2, prev2, params):
    n = p2.shape[0]
    const = lambda i: (0, 0)
    in_specs = [pl.BlockSpec((n, RWKV_PROJ), const), pl.BlockSpec((n, RWKV_PROJ), const)] + [
        pl.BlockSpec(t.shape, const) for t in params]
    out_shape = tuple(jax.ShapeDtypeStruct((n, RWKV_W), F32) for _ in range(8))
    out_specs = tuple(pl.BlockSpec((n, RWKV_W), const) for _ in range(8))
    return pl.pallas_call(
        _rwkv_prep_sample_kernel,
        grid=(1,),
        in_specs=in_specs, out_specs=out_specs, out_shape=out_shape,
        compiler_params=_cparams("arbitrary"),
        name="rwkv_prep_sample",
    )(p2, prev2, *params)


def _rwkv_chunk_kernel(r_ref, k_ref, v_ref, lw_ref, kk_ref, b_ref, s0_ref, o_ref, sout_ref, s_scr, *, chunk):
    C = chunk
    c_idx = pl.program_id(1)
    n_pairs = RWKV_HEADS // 2
    L = V7X_LANES

    @pl.when(c_idx == 0)
    def _():
        s_scr[...] = s0_ref[0]

    ti = lax.broadcasted_iota(jnp.int32, (C, C), 0)
    si = lax.broadcasted_iota(jnp.int32, (C, C), 1)
    tri = jnp.where(si <= ti, 1.0, 0.0).astype(BF16)
    lw = lw_ref[0]
    hi = lw.astype(BF16)
    rem = lw - hi.astype(F32)
    mid = rem.astype(BF16)
    lo = (rem - mid.astype(F32)).astype(BF16)
    cs = _dot(tri, hi) + _dot(tri, mid) + _dot(tri, lo)
    g = jnp.exp(cs)
    gprev = jnp.exp(cs - lw)
    ginv = jnp.exp(-cs)
    kap_all = kk_ref[0] * gprev
    rt_all = r_ref[0] * g
    kt_all = k_ref[0] * ginv
    bt_all = b_ref[0] * ginv
    v_all = v_ref[0]
    g_last = g[C - 1:C, :]

    lane = lax.broadcasted_iota(jnp.int32, (C, L), 1)
    in_h0 = lane < RWKV_HD

    def stack(x):
        z = jnp.zeros_like(x)
        return jnp.concatenate([jnp.where(in_h0, x, z), jnp.where(in_h0, z, x)], axis=0)

    rr = lax.broadcasted_iota(jnp.int32, (2 * C, 2 * C), 0)
    cc = lax.broadcasted_iota(jnp.int32, (2 * C, 2 * C), 1)
    same = (rr // C) == (cc // C)
    strict = same & (cc < rr)
    incl = same & (cc <= rr)
    eye = jnp.where(rr == cc, 1.0, 0.0).astype(F32)
    n_sq = max(int(math.ceil(math.log2(C))) - 1, 0)

    prs = range(n_pairs)
    sls = [slice(pr * L, (pr + 1) * L) for pr in prs]
    vs16 = [stack(v_all[:, sl]).astype(BF16) for sl in sls]
    lhs = [jnp.concatenate([stack(kap_all[:, sl]).astype(BF16), stack(rt_all[:, sl]).astype(BF16)], axis=0)
           for sl in sls]
    rhs = [jnp.concatenate([stack(kt_all[:, sl]).astype(BF16), stack(bt_all[:, sl]).astype(BF16)], axis=0)
           for sl in sls]
    A = [_dot_nt(lhs[pr], rhs[pr]) for pr in prs]
    S = [s_scr[pr] for pr in prs]
    KS = [_dot_nt(lhs[pr], S[pr].astype(BF16)) for pr in prs]
    zero = jnp.zeros((2 * C, 2 * C), F32)
    Lk16 = [jnp.where(strict, A[pr][0:2 * C, 0:2 * C], zero).astype(BF16) for pr in prs]
    Lb = [jnp.where(strict, A[pr][0:2 * C, 2 * C:4 * C], zero) for pr in prs]
    MkMb = [jnp.concatenate([jnp.where(incl, A[pr][2 * C:4 * C, 0:2 * C], zero),
                             -jnp.where(incl, A[pr][2 * C:4 * C, 2 * C:4 * C], zero)], axis=1).astype(BF16)
            for pr in prs]
    LkV = [_dot(Lk16[pr], vs16[pr]) for pr in prs]
    X = [eye - Lb[pr] for pr in prs]
    P16 = [Lb[pr].astype(BF16) for pr in prs]
    for _ in range(n_sq):
        P16 = [_dot(P16[pr], P16[pr]).astype(BF16) for pr in prs]
        X = [X[pr] + _dot(X[pr].astype(BF16), P16[pr]) for pr in prs]
    U16 = [_dot(X[pr].astype(BF16), (KS[pr][0:2 * C] + LkV[pr]).astype(BF16)).astype(BF16) for pr in prs]
    O = [KS[pr][2 * C:4 * C] + _dot(MkMb[pr], jnp.concatenate([vs16[pr], U16[pr]], axis=0)) for pr in prs]
    upd = [_dot_tn(jnp.concatenate([vs16[pr], -U16[pr]], axis=0), rhs[pr]) for pr in prs]
    for pr in prs:
        o_ref[0, :, sls[pr]] = O[pr][0:C] + O[pr][C:2 * C]
        s_scr[pr] = (S[pr] + upd[pr]) * g_last[:, sls[pr]]

    @pl.when(c_idx == pl.num_programs(1) - 1)
    def _():
        sout_ref[0] = s_scr[...]


def _pair_state(S):
    B = S.shape[0]
    S = S.reshape(B, RWKV_HEADS // 2, 2, RWKV_HD, RWKV_HD)
    z = jnp.zeros_like(S[:, :, 0])
    top = jnp.concatenate([S[:, :, 0], z], axis=-1)
    bot = jnp.concatenate([z, S[:, :, 1]], axis=-1)
    return jnp.concatenate([top, bot], axis=-2)


def _unpair_state(Sbd):
    B = Sbd.shape[0]
    a = Sbd[:, :, :RWKV_HD, :RWKV_HD]
    b = Sbd[:, :, RWKV_HD:, RWKV_HD:]
    return jnp.stack([a, b], axis=2).reshape(B, RWKV_HEADS, RWKV_HD, RWKV_HD)


def _rwkv_chunked(r, k, v, lw, kk, b, S0, chunk):
    B, T, _ = r.shape
    n_pairs = RWKV_HEADS // 2
    tok = pl.BlockSpec((1, chunk, RWKV_W), lambda bb, c: (bb, c, 0))
    st = pl.BlockSpec((1, n_pairs, V7X_LANES, V7X_LANES), lambda bb, c: (bb, 0, 0, 0))
    o, s_out = pl.pallas_call(
        functools.partial(_rwkv_chunk_kernel, chunk=chunk),
        grid=(B, T // chunk),
        in_specs=[tok] * 6 + [st],
        out_specs=(tok, st),
        out_shape=(jax.ShapeDtypeStruct((B, T, RWKV_W), F32),
                   jax.ShapeDtypeStruct((B, n_pairs, V7X_LANES, V7X_LANES), F32)),
        scratch_shapes=[pltpu.VMEM((n_pairs, V7X_LANES, V7X_LANES), F32)],
        compiler_params=_cparams("parallel", "arbitrary"),
        name="rwkv_chunk_%d" % chunk,
    )(r, k, v, lw, kk, b, _pair_state(S0))
    return o, _unpair_state(s_out)


def _layer_norm(x, g, b):
    mu = jnp.mean(x, axis=-1, keepdims=True)
    var = jnp.mean(jnp.square(x - mu), axis=-1, keepdims=True)
    return (x - mu) * lax.rsqrt(var + LN_EPS) * g + b


def _mix_out_kernel(ga_ref, gr_ref, att_ref, o_ref, bonus_ref, g_ref, x_ref, wout_ref, gng_ref, gnb_ref,
                    ln_g_ref, ln_b_ref, h32_ref, h16_ref):
    ones_bd = _head_ones(V7X_LANES, RWKV_HD)
    o = o_ref[...]
    mean = _segsum(o, ones_bd) * (1.0 / RWKV_HD)
    d = o - mean
    var = _segsum(d * d, ones_bd) * (1.0 / RWKV_HD)
    rw = d * lax.rsqrt(var + RWKV_GN_EPS) * gng_ref[...] + gnb_ref[...]
    rw = (rw + bonus_ref[...]) * g_ref[...]
    mixed_in = jax.nn.sigmoid(ga_ref[...]) * att_ref[...] + jax.nn.sigmoid(gr_ref[...]) * rw
    mixed = _dot(mixed_in.astype(BF16), wout_ref[...])
    h = _layer_norm(DN_ALPHA * x_ref[...] + mixed, ln_g_ref[...], ln_b_ref[...])
    h32_ref[...] = h
    h16_ref[...] = h.astype(BF16)


def _mix_out(gates, att, o, bonus, g, x2d, wout16, gn_g, gn_b, ln_g, ln_b, tm):
    n = x2d.shape[0]
    row = lambda i: (i, 0)
    const = lambda i: (0, 0)
    tok = pl.BlockSpec((tm, D_MODEL), row)
    vec = pl.BlockSpec((1, D_MODEL), const)
    return pl.pallas_call(
        _mix_out_kernel,
        grid=(n // tm,),
        in_specs=[tok, pl.BlockSpec((tm, D_MODEL), lambda i: (i, 1)), tok, tok, tok, tok, tok,
                  pl.BlockSpec((D_MODEL, D_MODEL), const), vec, vec, vec, vec],
        out_specs=(tok, tok),
        out_shape=(jax.ShapeDtypeStruct((n, D_MODEL), F32), jax.ShapeDtypeStruct((n, D_MODEL), BF16)),
        compiler_params=_cparams("parallel"),
        name="mix_out",
    )(gates, gates, att, o, bonus, g, x2d, wout16, gn_g.reshape(1, -1), gn_b.reshape(1, -1),
      ln_g.reshape(1, -1), ln_b.reshape(1, -1))


_STAIR = tuple((p, q) for p in range(PEER_TOPK) for q in range(PEER_TOPK // (p + 1)))
_STAIR_ROWS = -(-len(_STAIR) // V7X_SUBLANES) * V7X_SUBLANES


def _top_ranks(x, k):
    R = x.shape[0]
    row = lax.broadcasted_iota(jnp.int32, x.shape, 0)
    rank = jnp.full(x.shape, float(k), F32)
    vals = []
    for it in range(k):
        m = jnp.max(x, axis=0, keepdims=True)
        idx = jnp.min(jnp.where(x == m, row, R), axis=0, keepdims=True)
        hit = row == idx
        rank = jnp.where(hit, float(it), rank)
        x = jnp.where(hit, -jnp.inf, x)
        vals.append(m)
    return vals, rank


def _peer_select_kernel(h_ref, wq_ref, keys_ref, cut_ref, f1_ref, rank2_ref, e2_ref, cand_scr, sel_scr):
    q = _dot(h_ref[...], wq_ref[...]).astype(BF16)
    K = PEER_TOPK
    for h in range(PEER_HEADS):
        s = []
        for c in range(2):
            blk = (h * 2 + c) * PEER_HALF
            s.append(_dot_nt(keys_ref[h * 2 + c], q[:, blk:blk + PEER_HALF]))
        a1, rank1 = _top_ranks(s[0], K)
        a2, rank2 = _top_ranks(s[1], K)
        cand_scr[...] = jnp.full(cand_scr.shape, -jnp.inf, F32)
        for i, (p, qq) in enumerate(_STAIR):
            cand_scr[i:i + 1, :] = a1[p] + a2[qq]
        cand = cand_scr[...]
        _, crank = _top_ranks(cand, K)
        chosen = crank < float(K)
        z = jnp.sum(jnp.where(chosen, jnp.exp(cand - (a1[0] + a2[0])), 0.0), axis=0, keepdims=True)
        sel_scr[...] = jnp.where(chosen, 1.0, 0.0)
        cut = jnp.zeros(rank1.shape, F32)
        start = 0
        for p in range(K):
            n_p = K // (p + 1)
            cut_p = jnp.sum(sel_scr[start:start + n_p, :], axis=0, keepdims=True)
            cut = jnp.where(rank1 == float(p), cut_p, cut)
            start += n_p
        cut_ref[h] = cut
        f1_ref[h] = jnp.exp(s[0] - a1[0]) * (0.5 / z)
        rank2_ref[h] = rank2
        e2_ref[h] = jnp.exp(s[1] - a2[0])


def _peer_select(h16, wq16, keys16, tp):
    n = h16.shape[0]
    sel_shape = jax.ShapeDtypeStruct((PEER_HEADS, N_KEYS, n), F32)
    sel_spec = pl.BlockSpec((PEER_HEADS, N_KEYS, tp), lambda i: (0, 0, i))
    return pl.pallas_call(
        _peer_select_kernel,
        grid=(n // tp,),
        in_specs=[pl.BlockSpec((tp, D_MODEL), lambda i: (i, 0)),
                  pl.BlockSpec(wq16.shape, lambda i: (0, 0)),
                  pl.BlockSpec(keys16.shape, lambda i: (0, 0, 0))],
        out_specs=(sel_spec,) * 4,
        out_shape=(sel_shape,) * 4,
        scratch_shapes=[pltpu.VMEM((_STAIR_ROWS, tp), F32), pltpu.VMEM((_STAIR_ROWS, tp), F32)],
        compiler_params=_cparams("parallel"),
        name="peer_select",
    )(h16, wq16, keys16)


_PEER_I1_PER_BLOCK = 8
_PEER_I1_PER_CHUNK = 2
_SQRT_HALF = math.sqrt(0.5)


def _peer_dense_kernel(x_ref, u_ref, vt_ref, cut_ref, f1_ref, rank2_ref, e2_ref, h_ref, ln_g_ref, ln_b_ref,
                       y_ref, acc_scr, hid_scr, a_scr):
    e = pl.program_id(1)

    @pl.when(e == 0)
    def _():
        acc_scr[...] = jnp.zeros(acc_scr.shape, F32)
        hid_scr[...] = jnp.zeros(hid_scr.shape, F32)

    def stages(cur, prev):
        hid_scr[cur] = _dot_nt(u_ref[...], x_ref[...])
        for ch in range(_PEER_I1_PER_BLOCK // _PEER_I1_PER_CHUNK):
            for il in range(ch * _PEER_I1_PER_CHUNK, (ch + 1) * _PEER_I1_PER_CHUNK):
                rows = slice(il * N_KEYS, (il + 1) * N_KEYS)
                for tc in range(x_ref.shape[0] // V7X_LANES):
                    cols = slice(tc * V7X_LANES, (tc + 1) * V7X_LANES)
                    w = jnp.zeros((N_KEYS, V7X_LANES), F32)
                    for h in range(PEER_HEADS):
                        cutrow = cut_ref[h, il:il + 1, cols]
                        f1row = f1_ref[h, il:il + 1, cols]
                        w = w + jnp.where(rank2_ref[h, :, cols] < cutrow, e2_ref[h, :, cols] * f1row, 0.0)
                    hb = hid_scr[prev, rows, cols]
                    a_scr[rows, cols] = (w * (hb * (1.0 + lax.erf(hb * _SQRT_HALF)))).astype(BF16)
            crow = slice(ch * _PEER_I1_PER_CHUNK * N_KEYS, (ch + 1) * _PEER_I1_PER_CHUNK * N_KEYS)
            acc_scr[...] += _dot(vt_ref[:, crow], a_scr[crow, :])

    for parity in range(2):
        @pl.when(e % 2 == parity)
        def _(parity=parity):
            stages(parity, 1 - parity)

    @pl.when(e == pl.num_programs(1) - 1)
    def _():
        ff = acc_scr[...].T
        y_ref[...] = _layer_norm(DN_ALPHA * h_ref[...] + ff, ln_g_ref[...], ln_b_ref[...])


def _peer_dense(h32, h16, u16, vt16, sel, ln_g, ln_b, tt):
    n = h32.shape[0]
    assert tt % V7X_LANES == 0
    cut, f1, rank2, e2 = sel
    eb = _PEER_I1_PER_BLOCK * N_KEYS
    n_e = N_EXPERTS // eb
    tok = lambda i, e: (i, 0)
    const = lambda i, e: (0, 0)
    blk = lambda e, lag: jnp.clip(e - lag, 0, n_e - 1)
    return pl.pallas_call(
        _peer_dense_kernel,
        grid=(n // tt, n_e + 1),
        in_specs=[pl.BlockSpec((tt, D_MODEL), tok),
                  pl.BlockSpec((eb, D_MODEL), lambda i, e: (blk(e, 0), 0)),
                  pl.BlockSpec((D_MODEL, eb), lambda i, e: (0, blk(e, 1))),
                  pl.BlockSpec((PEER_HEADS, _PEER_I1_PER_BLOCK, tt), lambda i, e: (0, blk(e, 1), i)),
                  pl.BlockSpec((PEER_HEADS, _PEER_I1_PER_BLOCK, tt), lambda i, e: (0, blk(e, 1), i)),
                  pl.BlockSpec((PEER_HEADS, N_KEYS, tt), lambda i, e: (0, 0, i)),
                  pl.BlockSpec((PEER_HEADS, N_KEYS, tt), lambda i, e: (0, 0, i)),
                  pl.BlockSpec((tt, D_MODEL), tok),
                  pl.BlockSpec((1, D_MODEL), const), pl.BlockSpec((1, D_MODEL), const)],
        out_specs=pl.BlockSpec((tt, D_MODEL), tok),
        out_shape=jax.ShapeDtypeStruct((n, D_MODEL), F32),
        scratch_shapes=[pltpu.VMEM((D_MODEL, tt), F32), pltpu.VMEM((2, eb, tt), F32),
                        pltpu.VMEM((eb, tt), BF16)],
        compiler_params=_cparams("parallel", "arbitrary"),
        name="peer_dense",
    )(h16, u16, vt16, cut, f1, rank2, e2, h32, ln_g.reshape(1, -1), ln_b.reshape(1, -1))


_TM_PROMPT = 256
_TQ_PROMPT = 512
_RWKV_CHUNK = 64
_RWKV_CHUNK_SAMPLE = 16
_PAGES_PER_STEP = 4
_PEER_TP = 256
_PEER_TT = 512


def _layer_tail(gates, att2, o2, bonus2, g2, x2, lw, tm, tp, tt):
    h32, h16 = _mix_out(gates, att2, o2, bonus2, g2, x2, lw["wout16"], lw["rwkv_ln_g"], lw["rwkv_ln_b"],
                        lw["ln1_g"], lw["ln1_b"], tm)
    sel = _peer_select(h16, lw["wq16"], lw["keys16"], tp)
    return _peer_dense(h32, h16, lw["u16"], lw["vt16"], sel, lw["ln2_g"], lw["ln2_b"], tt)


def _prompt_layer(x, lw, rel_bias):
    B, T, _ = x.shape
    n = B * T
    x2 = x.reshape(n, D_MODEL)
    q16, k32, k16, v32, v16, gates, p = _in_proj(x2, lw["w_in16"], _TM_PROMPT)
    sh = lambda t: t.reshape(B, T, -1)
    vt16 = jnp.swapaxes(sh(v16), 1, 2)
    att = _prompt_attention(sh(q16), sh(k16), vt16, rel_bias, lw["lam_qk"], lw["attn_norm_g"], _TQ_PROMPT)
    p3 = sh(p)
    shift0 = jnp.zeros((B, RWKV_PROJ), F32)
    r, k, v, lgw, kk, b, g, bonus = _rwkv_prep_prompt(p3, shift0, lw["rwkv_params"], _TM_PROMPT)
    S0 = jnp.zeros((B, RWKV_HEADS, RWKV_HD, RWKV_HD), F32)
    o, S_new = _rwkv_chunked(r, k, v, lgw, kk, b, S0, _RWKV_CHUNK)
    fl = lambda t: t.reshape(n, -1)
    y = _layer_tail(gates, fl(att), fl(o), fl(bonus), fl(g), x2, lw, _TM_PROMPT, _PEER_TP, _PEER_TT)
    return (y.reshape(B, T, D_MODEL), k32.reshape(B, T, ATT_HEADS, 2 * ATT_HD), v32.reshape(B, T, ATT_HEADS, ATT_VD),
            S_new, p3[:, -1, :])


def _sample_layer(x, lw, rel_bias, cache_k, cache_v, layer, page_table, shift0, S0):
    DB, S, _ = x.shape
    n = DB * S
    x2 = x.reshape(n, D_MODEL)
    q16, k32, k16, v32, v16, gates, p = _in_proj(x2, lw["w_in16"], n)
    sh = lambda t: t.reshape(DB, S, -1)
    att = _sample_attention(sh(q16), sh(k32), sh(v32), cache_k, cache_v, layer, page_table, rel_bias,
                            lw["lam_qk"], lw["attn_norm_g"], _PAGES_PER_STEP)
    p3 = sh(p)
    prev = jnp.concatenate([shift0[:, None, :], p3[:, :-1, :]], axis=1).reshape(n, RWKV_PROJ)
    r, k, v, lgw, kk, b, g, bonus = _rwkv_prep_sample(p, prev, lw["rwkv_params"])
    pad = lambda t: jnp.pad(sh(t), ((0, 0), (0, _RWKV_CHUNK_SAMPLE - S), (0, 0)))
    o, S_new = _rwkv_chunked(pad(r), pad(k), pad(v), pad(lgw), pad(kk), pad(b), S0, _RWKV_CHUNK_SAMPLE)
    o2 = o[:, :S, :].reshape(n, RWKV_W)
    y = _layer_tail(gates, att.reshape(n, ATT_V), o2, bonus, g, x2, lw, n, n, n)
    return (y.reshape(DB, S, D_MODEL), k32.reshape(DB, S, ATT_HEADS, 2 * ATT_HD), v32.reshape(DB, S, ATT_HEADS, ATT_VD),
            S_new, p3[:, -1, :])


def kernel(x_prompt, x_sample, cache_k, cache_v, state_wkv, state_shift, page_table, w_in, w_out, lam_qk, attn_norm_g, shift_mu, rwkv_w0, rwkv_w2, rwkv_a0, rwkv_a2, rwkv_g2, rwkv_kk, rwkv_ka, rwkv_rk, rwkv_ln_g, rwkv_ln_b, ln1_g, ln1_b, ln2_g, ln2_b, peer_wq, peer_keys, peer_u, peer_v, rel_bias):
    assert w_in.shape[0] == DEPTH == 1
    yp, ys = x_prompt, x_sample
    outs = [[] for _ in range(8)]
    for l in range(DEPTH):
        lw = {
            "w_in16": w_in[l].astype(BF16), "wout16": w_out[l].astype(BF16),
            "lam_qk": lam_qk[l], "attn_norm_g": attn_norm_g[l],
            "rwkv_params": _rwkv_param_list(shift_mu[l], rwkv_w0[l], rwkv_w2[l], rwkv_a0[l], rwkv_a2[l],
                                            rwkv_g2[l], rwkv_kk[l], rwkv_ka[l], rwkv_rk[l]),
            "rwkv_ln_g": rwkv_ln_g[l], "rwkv_ln_b": rwkv_ln_b[l],
            "ln1_g": ln1_g[l], "ln1_b": ln1_b[l], "ln2_g": ln2_g[l], "ln2_b": ln2_b[l],
            "wq16": peer_wq[l].astype(BF16),
            "keys16": peer_keys[l].reshape(PEER_HEADS * 2, N_KEYS, PEER_HALF).astype(BF16),
            "u16": peer_u[l].astype(BF16), "vt16": peer_v[l].T.astype(BF16),
        }
        yp, kp, vp, Sp, shp = _prompt_layer(yp, lw, rel_bias)
        ys, ksm, vsm, Ss, shs = _sample_layer(ys, lw, rel_bias, cache_k, cache_v, l, page_table,
                                              state_shift[l], state_wkv[l])
        for lst, val in zip(outs, (kp, vp, ksm, vsm, Sp, Ss, shp, shs)):
            lst.append(val)
    stacked = tuple(jnp.stack(lst, axis=0) for lst in outs)
    return (yp, ys) + stacked
```

```python
import functools
import math

import jax
import jax.numpy as jnp
from jax import lax
from jax.experimental import pallas as pl
from jax.experimental.pallas import tpu as pltpu

D_MODEL = 1024
PAGE_SIZE = 128
ATT_HD = 64
ATT_HEADS = D_MODEL // (2 * ATT_HD)
ATT_VD = 2 * ATT_HD
ATT_QK = ATT_HEADS * 2 * ATT_HD
ATT_V = ATT_HEADS * ATT_VD
N_BUCKETS = 32
MAX_DISTANCE = 128
ATT_EPS = 1e-5
RWKV_HD = 64
RWKV_HEADS = D_MODEL // RWKV_HD
RWKV_W = RWKV_HEADS * RWKV_HD
DECAY_LORA = 64
AAA_LORA = 64
GATE_LORA = 128
RWKV_PROJ = 3 * RWKV_W + DECAY_LORA + AAA_LORA + GATE_LORA
DECAY_SCALE = 0.606531
RWKV_GN_EPS = 64e-5
O_K = ATT_QK
O_V = O_K + ATT_QK
O_GA = O_V + ATT_V
O_GR = O_GA + D_MODEL
O_RW = O_GR + D_MODEL
N_IN = O_RW + RWKV_PROJ
PEER_HEADS = 8
N_KEYS = 128
N_EXPERTS = N_KEYS * N_KEYS
PEER_QDIM = 256
PEER_HALF = PEER_QDIM // 2
PEER_TOPK = 16
DEPTH = 1
DN_ALPHA = (2.0 * DEPTH) ** 0.25
LN_EPS = 1e-5
LAMBDA_INIT = 0.8 - 0.6 * math.exp(-0.3 * 0)

V7X_LANES = 128
V7X_SUBLANES = 8
V7X_VMEM_LIMIT_BYTES = 56 * 1024 * 1024

NEG_INF = -1e30
LOG2E = math.log2(math.e)
BF16 = jnp.bfloat16
F32 = jnp.float32

NT_DIMS = (((1,), (1,)), ((), ()))
TN_DIMS = (((0,), (0,)), ((), ()))


def _cparams(*sem, flags=None):
    return pltpu.CompilerParams(dimension_semantics=sem, vmem_limit_bytes=V7X_VMEM_LIMIT_BYTES, flags=flags)


def _dot(a, b):
    return jnp.dot(a, b, preferred_element_type=F32)


def _dot_nt(a, b):
    return lax.dot_general(a, b, NT_DIMS, preferred_element_type=F32)


def _dot_tn(a, b):
    return lax.dot_general(a, b, TN_DIMS, preferred_element_type=F32)


def _split_dot(x, w_bf16):
    hi = x.astype(BF16)
    lo = (x - hi.astype(F32)).astype(BF16)
    return _dot(hi, w_bf16) + _dot(lo, w_bf16)


_P_CHUNKS = ((0, 1024), (1024, 2048), (2048, 3072), (3072, RWKV_PROJ))


def _in_proj_kernel(x_ref, w_ref, q16_ref, k32_ref, k16_ref, v32_ref, v16_ref, gates_ref, p_ref):
    x = x_ref[...].astype(BF16)

    def mm(lo, hi):
        return _dot(x, w_ref[:, lo:hi])

    q16_ref[...] = (mm(0, O_K) * (ATT_HD ** -0.5 * LOG2E)).astype(BF16)
    k = mm(O_K, O_V)
    k32_ref[...] = k
    k16_ref[...] = k.astype(BF16)
    v = mm(O_V, O_GA)
    v32_ref[...] = v
    v16_ref[...] = v.astype(BF16)
    gates_ref[:, 0:D_MODEL] = mm(O_GA, O_GR)
    gates_ref[:, D_MODEL:2 * D_MODEL] = mm(O_GR, O_RW)
    for lo, hi in _P_CHUNKS:
        p_ref[:, lo:hi] = mm(O_RW + lo, O_RW + hi)


def _in_proj(x2d, w_bf16, tm):
    n = x2d.shape[0]
    row = lambda i: (i, 0)
    full = lambda i: (0, 0)
    out_shape = (
        jax.ShapeDtypeStruct((n, ATT_QK), BF16),
        jax.ShapeDtypeStruct((n, ATT_QK), F32),
        jax.ShapeDtypeStruct((n, ATT_QK), BF16),
        jax.ShapeDtypeStruct((n, ATT_V), F32),
        jax.ShapeDtypeStruct((n, ATT_V), BF16),
        jax.ShapeDtypeStruct((n, 2 * D_MODEL), F32),
        jax.ShapeDtypeStruct((n, RWKV_PROJ), F32),
    )
    out_specs = tuple(pl.BlockSpec((tm, s.shape[1]), row) for s in out_shape)
    return pl.pallas_call(
        _in_proj_kernel,
        grid=(n // tm,),
        in_specs=[pl.BlockSpec((tm, D_MODEL), row),
                  pl.BlockSpec((D_MODEL, N_IN), full, pipeline_mode=pl.Buffered(1))],
        out_specs=out_specs,
        out_shape=out_shape,
        compiler_params=_cparams("parallel"),
        name="in_proj",
    )(x2d, w_bf16)


def _t5_bucket(rel):
    n = jnp.maximum(rel, 0)
    max_exact = N_BUCKETS // 2
    nf = jnp.maximum(n, 1).astype(F32)
    large = max_exact + (jnp.log(nf / max_exact) / math.log(MAX_DISTANCE / max_exact)
                         * (N_BUCKETS - max_exact)).astype(jnp.int32)
    large = jnp.minimum(large, N_BUCKETS - 1)
    return jnp.where(n < max_exact, n, large)


def _shifted_bias(rel_bias, dist):
    far = lax.dynamic_index_in_dim(rel_bias, _t5_bucket(jnp.int32(2 * MAX_DISTANCE)), 0, keepdims=False)
    table = (rel_bias - far[None, :]) * LOG2E
    bucket = _t5_bucket(dist)[None]
    expand = (slice(None),) + (None,) * dist.ndim
    out = jnp.zeros((rel_bias.shape[1],) + dist.shape, F32)
    for b in range(N_BUCKETS):
        out = jnp.where(bucket == b, table[b][expand], out)
    return out


def _per_row_head(tiles, row_h):
    out = jnp.zeros(tiles.shape[1:], F32)
    for h in range(tiles.shape[0]):
        out = jnp.where(row_h[:, None] == h, tiles[h], out)
    return out


def _lambda_value(lq):
    s01 = jnp.sum(lq[0:1] * lq[1:2], axis=-1, keepdims=True)
    s23 = jnp.sum(lq[2:3] * lq[3:4], axis=-1, keepdims=True)
    return jnp.exp(s01) - jnp.exp(s23) + LAMBDA_INIT


def _rms_scale(o):
    return o * lax.rsqrt(jnp.mean(o * o, axis=-1, keepdims=True) + ATT_EPS)


def _prompt_attn_kernel(q_ref, k_ref, vt_ref, bias_ref, lam_ref, g_ref, o_ref, m_ref, l_ref, acc_ref,
                        sa_ref, sb_ref, *, tq):
    i = pl.program_id(2)
    tk = tq // 2
    lane = lax.broadcasted_iota(jnp.int32, (tq, 2 * ATT_HD), 1)
    q = q_ref[0]
    zero = jnp.zeros_like(q)
    qm = (jnp.where(lane < ATT_HD, q, zero), jnp.where(lane >= ATT_HD, q, zero))

    m_ref[...] = jnp.full(m_ref.shape, NEG_INF, F32)
    l_ref[...] = jnp.zeros(l_ref.shape, F32)
    acc_ref[...] = jnp.zeros(acc_ref.shape, F32)

    def scores(b, s_ref):
        kb = k_ref[0, pl.ds(pl.multiple_of(b * tk, tk), tk), :]
        for c in range(2):
            s_ref[c] = _dot_nt(kb, qm[c])

    def consume(b, s_ref, add_t):
        vt = vt_ref[0, :, pl.ds(pl.multiple_of(b * tk, tk), tk)]
        p16, alphas = [], []
        for c in range(2):
            sc = s_ref[c] if add_t is None else s_ref[c] + add_t
            m_old = m_ref[c]
            m_new = jnp.maximum(m_old, jnp.max(sc, axis=0, keepdims=True))
            alpha = jnp.exp2(m_old - m_new)
            p = jnp.exp2(sc - m_new)
            l_ref[c] = alpha * l_ref[c] + jnp.sum(p, axis=0, keepdims=True)
            m_ref[c] = m_new
            p16.append(p.astype(BF16))
            alphas.append(alpha)
        for c in range(2):
            acc_ref[c] = alphas[c] * acc_ref[c] + _dot(vt, p16[c])

    scores(0, sa_ref)

    def far_body(p, carry):
        scores(2 * p + 1, sb_ref)
        consume(2 * p, sa_ref, None)
        scores(2 * p + 2, sa_ref)
        consume(2 * p + 1, sb_ref, None)
        return carry

    lax.fori_loop(0, jnp.maximum(i - 1, 0), far_body, 0)

    @pl.when(i > 0)
    def _():
        scores(2 * i - 1, sb_ref)
        consume(2 * i - 2, sa_ref, None)
        scores(2 * i, sa_ref)
        consume(2 * i - 1, sb_ref, bias_ref[0, 0])

    scores(2 * i + 1, sb_ref)
    consume(2 * i, sa_ref, bias_ref[0, 1])
    consume(2 * i + 1, sb_ref, bias_ref[0, 2])

    lam = _lambda_value(lam_ref[...])
    ot = acc_ref[0] / l_ref[0] - lam * (acc_ref[1] / l_ref[1])
    ot = ot * lax.rsqrt(jnp.mean(ot * ot, axis=0, keepdims=True) + ATT_EPS)
    o_ref[0] = ot.T * (g_ref[...] * (1.0 - LAMBDA_INIT))


def _prompt_attention(q16, k16, vt16, rel_bias, lam_qk, attn_norm_g, tq):
    B, T, _ = q16.shape
    tk = tq // 2
    assert tk >= MAX_DISTANCE
    diff_t = jnp.arange(tq, dtype=jnp.int32)[None, :] - jnp.arange(tk, dtype=jnp.int32)[:, None]
    tiles = []
    for off in (tk, 0, -tk):
        dist = diff_t + off
        tiles.append(jnp.where(dist >= 0, _shifted_bias(rel_bias, dist), NEG_INF))
    bias = jnp.stack(tiles, axis=1).astype(F32)
    kern = functools.partial(_prompt_attn_kernel, tq=tq)
    return pl.pallas_call(
        kern,
        grid=(B, ATT_HEADS, T // tq),
        in_specs=[
            pl.BlockSpec((1, tq, 2 * ATT_HD), lambda b, h, i: (b, i, h)),
            pl.BlockSpec((1, T, 2 * ATT_HD), lambda b, h, i: (b, 0, h)),
            pl.BlockSpec((1, ATT_VD, T), lambda b, h, i: (b, h, 0)),
            pl.BlockSpec((1, 3, tk, tq), lambda b, h, i: (h, 0, 0, 0)),
            pl.BlockSpec((4, ATT_HD), lambda b, h, i: (0, 0)),
            pl.BlockSpec((1, ATT_VD), lambda b, h, i: (0, h)),
        ],
        out_specs=pl.BlockSpec((1, tq, ATT_VD), lambda b, h, i: (b, i, h)),
        out_shape=jax.ShapeDtypeStruct((B, T, ATT_V), F32),
        scratch_shapes=[pltpu.VMEM((2, 1, tq), F32), pltpu.VMEM((2, 1, tq), F32),
                        pltpu.VMEM((2, ATT_VD, tq), F32),
                        pltpu.VMEM((2, tk, tq), F32), pltpu.VMEM((2, tk, tq), F32)],
        compiler_params=_cparams("parallel", "parallel", "arbitrary"),
        name="prompt_attention",
    )(q16, k16, vt16, bias, lam_qk, attn_norm_g.reshape(1, ATT_V))


def _sample_attn_kernel(pt_ref, q_ref, plain_ref, last_ref, new_ref, knew_ref, vnew_ref, lam_ref, g_ref,
                        *rest, pages_per_step):
    k_refs = rest[:pages_per_step]
    v_refs = rest[pages_per_step:2 * pages_per_step]
    o_ref, m_ref, l_ref, acc_ref = rest[2 * pages_per_step:]
    j = pl.program_id(1)
    last = pl.num_programs(1) - 1
    q = q_ref[0]

    @pl.when(j == 0)
    def _():
        m_ref[...] = jnp.full(m_ref.shape, NEG_INF, F32)
        l_ref[...] = jnp.zeros(l_ref.shape, F32)
        acc_ref[...] = jnp.zeros(acc_ref.shape, F32)

    def update(s, v2d):
        m_old = m_ref[...]
        m_new = jnp.maximum(m_old, jnp.max(s, axis=-1, keepdims=True))
        alpha = jnp.exp2(m_old - m_new)
        p = jnp.exp2(s - m_new)
        l_ref[...] = alpha * l_ref[...] + jnp.sum(p, axis=-1, keepdims=True)
        acc_ref[...] = alpha * acc_ref[...] + _dot(p.astype(BF16), v2d)
        m_ref[...] = m_new

    rows2d = PAGE_SIZE * ATT_HEADS
    k_all = jnp.concatenate([r[...].reshape(rows2d, 2 * ATT_HD).astype(BF16) for r in k_refs], axis=0)
    v_all = jnp.concatenate([r[...].reshape(rows2d, ATT_VD).astype(BF16) for r in v_refs], axis=0)
    add = jnp.concatenate([plain_ref[...]] * (pages_per_step - 1) + [last_ref[0]], axis=1)
    update(_dot_nt(q, k_all) + add, v_all)

    @pl.when(j == last)
    def _():
        n_new = knew_ref.shape[1]
        k2d = knew_ref[0].reshape(n_new * ATT_HEADS, 2 * ATT_HD).astype(BF16)
        v2d = vnew_ref[0].reshape(n_new * ATT_HEADS, ATT_VD).astype(BF16)
        update(_dot_nt(q, k2d) + new_ref[...], v2d)
        half = acc_ref.shape[0] // 2
        o = acc_ref[...] / l_ref[...]
        lam = _lambda_value(lam_ref[...])
        diff = o[0:half] - lam * o[half:2 * half]
        o_ref[0] = _rms_scale(diff) * (g_ref[...] * (1.0 - LAMBDA_INIT))


def _sample_attention(q16, k32, v32, cache_k, cache_v, layer, page_table, rel_bias, lam_qk, attn_norm_g,
                      pages_per_step):
    DB, S, _ = q16.shape
    n_pages = page_table.shape[1]
    past = n_pages * PAGE_SIZE
    H = ATT_HEADS
    rows = 2 * S * H
    q5 = q16.reshape(DB, S, H, 2, ATT_HD)
    qrows = jnp.stack([jnp.concatenate([q5[:, :, :, 0], jnp.zeros_like(q5[:, :, :, 0])], axis=-1),
                       jnp.concatenate([jnp.zeros_like(q5[:, :, :, 1]), q5[:, :, :, 1]], axis=-1)], axis=1)
    qrows = qrows.reshape(DB, rows, 2 * ATT_HD)
    row_s = (jnp.arange(rows, dtype=jnp.int32) // H) % S
    row_h = jnp.arange(rows, dtype=jnp.int32) % H
    col_t = jnp.arange(PAGE_SIZE * H, dtype=jnp.int32) // H
    col_h = jnp.arange(PAGE_SIZE * H, dtype=jnp.int32) % H
    same_head = row_h[:, None] == col_h[None, :]
    plain = jnp.where(same_head, 0.0, NEG_INF).astype(F32)
    dist_last = PAGE_SIZE + row_s[:, None] - col_t[None, :]
    b_last = _per_row_head(_shifted_bias(rel_bias, dist_last), row_h)
    last_tile = jnp.where(same_head, b_last, NEG_INF).astype(F32)
    tiles = jnp.stack([plain, last_tile], axis=0)
    ncol_t = jnp.arange(S * H, dtype=jnp.int32) // H
    ncol_h = jnp.arange(S * H, dtype=jnp.int32) % H
    dist_new = row_s[:, None] - ncol_t[None, :]
    b_new = _per_row_head(_shifted_bias(rel_bias, dist_new), row_h)
    ok_new = (row_h[:, None] == ncol_h[None, :]) & (dist_new >= 0)
    new_tile = jnp.where(ok_new, b_new, NEG_INF).astype(F32)
    g_tile = jnp.tile(attn_norm_g.reshape(H, ATT_VD), (S, 1))

    assert n_pages % pages_per_step == 0
    n_steps = n_pages // pages_per_step
    kern = functools.partial(_sample_attn_kernel, pages_per_step=pages_per_step)

    def page_spec(g):
        return pl.BlockSpec((None, None, PAGE_SIZE, H, 2 * ATT_HD),
                            lambda b, j, pt, g=g: (layer, pt[b, j * pages_per_step + g], 0, 0, 0))

    grid_spec = pltpu.PrefetchScalarGridSpec(
        num_scalar_prefetch=1,
        grid=(DB, n_steps),
        in_specs=[
            pl.BlockSpec((1, rows, 2 * ATT_HD), lambda b, j, pt: (b, 0, 0)),
            pl.BlockSpec((None, rows, PAGE_SIZE * H), lambda b, j, pt: (0, 0, 0)),
            pl.BlockSpec((1, rows, PAGE_SIZE * H), lambda b, j, pt: (jnp.where(j == n_steps - 1, 1, 0), 0, 0)),
            pl.BlockSpec((rows, S * H), lambda b, j, pt: (0, 0)),
            pl.BlockSpec((1, S, H, 2 * ATT_HD), lambda b, j, pt: (b, 0, 0, 0)),
            pl.BlockSpec((1, S, H, ATT_VD), lambda b, j, pt: (b, 0, 0, 0)),
            pl.BlockSpec((4, ATT_HD), lambda b, j, pt: (0, 0)),
            pl.BlockSpec((S * H, ATT_VD), lambda b, j, pt: (0, 0)),
        ] + [page_spec(g) for g in range(pages_per_step)] + [page_spec(g) for g in range(pages_per_step)],
        out_specs=pl.BlockSpec((1, S * H, ATT_VD), lambda b, j, pt: (b, 0, 0)),
        scratch_shapes=[pltpu.VMEM((rows, 1), F32), pltpu.VMEM((rows, 1), F32), pltpu.VMEM((rows, ATT_VD), F32)],
    )
    out = pl.pallas_call(
        kern,
        grid_spec=grid_spec,
        out_shape=jax.ShapeDtypeStruct((DB, S * H, ATT_VD), F32),
        compiler_params=_cparams("parallel", "arbitrary"),
        name="sample_attention",
    )(page_table, qrows, tiles, tiles, new_tile,
      k32.reshape(DB, S, H, 2 * ATT_HD), v32.reshape(DB, S, H, ATT_VD), lam_qk, g_tile,
      *([cache_k] * pages_per_step), *([cache_v] * pages_per_step))
    return out.reshape(DB, S, ATT_V)


def _head_ones(n, seg):
    r = lax.broadcasted_iota(jnp.int32, (n, n), 0) // seg
    c = lax.broadcasted_iota(jnp.int32, (n, n), 1) // seg
    return jnp.where(r == c, 1.0, 0.0).astype(BF16)


def _segsum(x, ones_bd):
    outs = []
    for g in range(x.shape[1] // V7X_LANES):
        outs.append(_split_dot(x[:, g * V7X_LANES:(g + 1) * V7X_LANES], ones_bd))
    return jnp.concatenate(outs, axis=1)


def _rwkv_prep_core(p, prev, mu_ref, w0_ref, w2_ref, a0_ref, a2_ref, g2_ref, kks_ref, ka_ref, rk_ref, outs):
    r_ref, k_ref, v_ref, lw_ref, kk_ref, b_ref, g_ref, bonus_ref = outs
    W = RWKV_W
    xs = p + (prev - p) * mu_ref[...]
    r = xs[:, 0:W]
    k = xs[:, W:2 * W]
    v = xs[:, 2 * W:3 * W]
    wa = xs[:, 3 * W:3 * W + DECAY_LORA + AAA_LORA]
    gd = xs[:, 3 * W + DECAY_LORA + AAA_LORA:RWKV_PROJ]
    logw = -DECAY_SCALE * jax.nn.sigmoid(w0_ref[...] + _dot(jnp.tanh(wa).astype(BF16), w2_ref[...]))
    a = jax.nn.sigmoid(a0_ref[...] + _dot(wa.astype(BF16), a2_ref[...]))
    g = _dot(jax.nn.sigmoid(gd).astype(BF16), g2_ref[...])
    ones_bd = _head_ones(V7X_LANES, RWKV_HD)
    kk = k * kks_ref[...]
    kk = kk / jnp.maximum(jnp.sqrt(_segsum(kk * kk, ones_bd)), 1e-12)
    k = k * (1.0 + (a - 1.0) * ka_ref[...])
    r_ref[...] = r
    k_ref[...] = k
    v_ref[...] = v
    lw_ref[...] = logw
    kk_ref[...] = kk
    b_ref[...] = kk * a
    g_ref[...] = g
    bonus_ref[...] = _segsum(r * k * rk_ref[...], ones_bd) * v


def _rwkv_prep_prompt_kernel(p_ref, tail_ref, shift_ref, *rest):
    params, outs = rest[:9], rest[9:]
    i = pl.program_id(1)
    p = p_ref[0]
    tm = p.shape[0]
    first = jnp.where(i == 0, shift_ref[0], tail_ref[0, V7X_SUBLANES - 1:V7X_SUBLANES, :])
    row = lax.broadcasted_iota(jnp.int32, p.shape, 0)
    prev = jnp.where(row == 0, first, pltpu.roll(p, 1, 0))
    _rwkv_prep_core(p, prev, *params, [o.at[0] for o in outs])


def _rwkv_prep_sample_kernel(p_ref, prev_ref, *rest):
    params, outs = rest[:9], rest[9:]
    _rwkv_prep_core(p_ref[...], prev_ref[...], *params, outs)


def _rwkv_param_list(shift_mu, rwkv_w0, rwkv_w2, rwkv_a0, rwkv_a2, rwkv_g2, rwkv_kk, rwkv_ka, rwkv_rk):
    zeros = jnp.zeros((AAA_LORA, RWKV_W), F32)
    w2p = jnp.concatenate([rwkv_w2, zeros], axis=0).astype(BF16)
    a2p = jnp.concatenate([jnp.zeros((DECAY_LORA, RWKV_W), F32), rwkv_a2], axis=0).astype(BF16)
    row = lambda t: t.reshape(1, -1)
    return [row(shift_mu), row(rwkv_w0), w2p, row(rwkv_a0), a2p, rwkv_g2.astype(BF16),
            row(rwkv_kk), row(rwkv_ka), row(rwkv_rk)]


def _rwkv_prep_prompt(p3, shift0, params, tm):
    B, T, _ = p3.shape
    const = lambda b, i: (0, 0)
    tail_blocks = tm // V7X_SUBLANES
    in_specs = [
        pl.BlockSpec((1, tm, RWKV_PROJ), lambda b, i: (b, i, 0)),
        pl.BlockSpec((1, V7X_SUBLANES, RWKV_PROJ), lambda b, i: (b, jnp.maximum(i * tail_blocks - 1, 0), 0)),
        pl.BlockSpec((1, 1, RWKV_PROJ), lambda b, i: (b, 0, 0)),
    ] + [pl.BlockSpec(t.shape, const) for t in params]
    out_shape = tuple(jax.ShapeDtypeStruct((B, T, RWKV_W), F32) for _ in range(8))
    out_specs = tuple(pl.BlockSpec((1, tm, RWKV_W), lambda b, i: (b, i, 0)) for _ in range(8))
    return pl.pallas_call(
        _rwkv_prep_prompt_kernel,
        grid=(B, T // tm),
        in_specs=in_specs, out_specs=out_specs, out_shape=out_shape,
        compiler_params=_cparams("parallel", "parallel"),
        name="rwkv_prep_prompt",
    )(p3, p3, shift0.reshape(B, 1, RWKV_PROJ), *params)


def _rwkv_prep_sample(p2, prev2, params):
    n = p2.shape[0]
    const = lambda i: (0, 0)
    in_specs = [pl.BlockSpec((n, RWKV_PROJ), const), pl.BlockSpec((n, RWKV_PROJ), const)] + [
        pl.BlockSpec(t.shape, const) for t in params]
    out_shape = tuple(jax.ShapeDtypeStruct((n, RWKV_W), F32) for _ in range(8))
    out_specs = tuple(pl.BlockSpec((n, RWKV_W), const) for _ in range(8))
    return pl.pallas_call(
        _rwkv_prep_sample_kernel,
        grid=(1,),
        in_specs=in_specs, out_specs=out_specs, out_shape=out_shape,
        compiler_params=_cparams("arbitrary"),
        name="rwkv_prep_sample",
    )(p2, prev2, *params)


def _rwkv_chunk_kernel(r_ref, k_ref, v_ref, lw_ref, kk_ref, b_ref, s0_ref, o_ref, sout_ref, s_scr, *, chunk):
    C = chunk
    c_idx = pl.program_id(1)
    n_pairs = RWKV_HEADS // 2
    L = V7X_LANES

    @pl.when(c_idx == 0)
    def _():
        s_scr[...] = s0_ref[0]

    ti = lax.broadcasted_iota(jnp.int32, (C, C), 0)
    si = lax.broadcasted_iota(jnp.int32, (C, C), 1)
    tri = jnp.where(si <= ti, 1.0, 0.0).astype(BF16)
    lw = lw_ref[0]
    hi = lw.astype(BF16)
    rem = lw - hi.astype(F32)
    mid = rem.astype(BF16)
    lo = (rem - mid.astype(F32)).astype(BF16)
    cs = _dot(tri, hi) + _dot(tri, mid) + _dot(tri, lo)
    g = jnp.exp(cs)
    gprev = jnp.exp(cs - lw)
    ginv = jnp.exp(-cs)
    kap_all = kk_ref[0] * gprev
    rt_all = r_ref[0] * g
    kt_all = k_ref[0] * ginv
    bt_all = b_ref[0] * ginv
    v_all = v_ref[0]
    g_last = g[C - 1:C, :]

    lane = lax.broadcasted_iota(jnp.int32, (C, L), 1)
    in_h0 = lane < RWKV_HD

    def stack(x):
        z = jnp.zeros_like(x)
        return jnp.concatenate([jnp.where(in_h0, x, z), jnp.where(in_h0, z, x)], axis=0)

    rr = lax.broadcasted_iota(jnp.int32, (2 * C, 2 * C), 0)
    cc = lax.broadcasted_iota(jnp.int32, (2 * C, 2 * C), 1)
    same = (rr // C) == (cc // C)
    strict = same & (cc < rr)
    incl = same & (cc <= rr)
    eye = jnp.where(rr == cc, 1.0, 0.0).astype(F32)
    n_sq = max(int(math.ceil(math.log2(C))) - 1, 0)

    prs = range(n_pairs)
    sls = [slice(pr * L, (pr + 1) * L) for pr in prs]
    vs16 = [stack(v_all[:, sl]).astype(BF16) for sl in sls]
    lhs = [jnp.concatenate([stack(kap_all[:, sl]).astype(BF16), stack(rt_all[:, sl]).astype(BF16)], axis=0)
           for sl in sls]
    rhs = [jnp.concatenate([stack(kt_all[:, sl]).astype(BF16), stack(bt_all[:, sl]).astype(BF16)], axis=0)
           for sl in sls]
    A = [_dot_nt(lhs[pr], rhs[pr]) for pr in prs]
    S = [s_scr[pr] for pr in prs]
    KS = [_dot_nt(lhs[pr], S[pr].astype(BF16)) for pr in prs]
    zero = jnp.zeros((2 * C, 2 * C), F32)
    Lk16 = [jnp.where(strict, A[pr][0:2 * C, 0:2 * C], zero).astype(BF16) for pr in prs]
    Lb = [jnp.where(strict, A[pr][0:2 * C, 2 * C:4 * C], zero) for pr in prs]
    MkMb = [jnp.concatenate([jnp.where(incl, A[pr][2 * C:4 * C, 0:2 * C], zero),
                             -jnp.where(incl, A[pr][2 * C:4 * C, 2 * C:4 * C], zero)], axis=1).astype(BF16)
            for pr in prs]
    LkV = [_dot(Lk16[pr], vs16[pr]) for pr in prs]
    X = [eye - Lb[pr] for pr in prs]
    P16 = [Lb[pr].astype(BF16) for pr in prs]
    for _ in range(n_sq):
        P16 = [_dot(P16[pr], P16[pr]).astype(BF16) for pr in prs]
        X = [X[pr] + _dot(X[pr].astype(BF16), P16[pr]) for pr in prs]
    U16 = [_dot(X[pr].astype(BF16), (KS[pr][0:2 * C] + LkV[pr]).astype(BF16)).astype(BF16) for pr in prs]
    O = [KS[pr][2 * C:4 * C] + _dot(MkMb[pr], jnp.concatenate([vs16[pr], U16[pr]], axis=0)) for pr in prs]
    upd = [_dot_tn(jnp.concatenate([vs16[pr], -U16[pr]], axis=0), rhs[pr]) for pr in prs]
    for pr in prs:
        o_ref[0, :, sls[pr]] = O[pr][0:C] + O[pr][C:2 * C]
        s_scr[pr] = (S[pr] + upd[pr]) * g_last[:, sls[pr]]

    @pl.when(c_idx == pl.num_programs(1) - 1)
    def _():
        sout_ref[0] = s_scr[...]


def _pair_state(S):
    B = S.shape[0]
    S = S.reshape(B, RWKV_HEADS // 2, 2, RWKV_HD, RWKV_HD)
    z = jnp.zeros_like(S[:, :, 0])
    top = jnp.concatenate([S[:, :, 0], z], axis=-1)
    bot = jnp.concatenate([z, S[:, :, 1]], axis=-1)
    return jnp.concatenate([top, bot], axis=-2)


def _unpair_state(Sbd):
    B = Sbd.shape[0]
    a = Sbd[:, :, :RWKV_HD, :RWKV_HD]
    b = Sbd[:, :, RWKV_HD:, RWKV_HD:]
    return jnp.stack([a, b], axis=2).reshape(B, RWKV_HEADS, RWKV_HD, RWKV_HD)


def _rwkv_chunked(r, k, v, lw, kk, b, S0, chunk):
    B, T, _ = r.shape
    n_pairs = RWKV_HEADS // 2
    tok = pl.BlockSpec((1, chunk, RWKV_W), lambda bb, c: (bb, c, 0))
    st = pl.BlockSpec((1, n_pairs, V7X_LANES, V7X_LANES), lambda bb, c: (bb, 0, 0, 0))
    o, s_out = pl.pallas_call(
        functools.partial(_rwkv_chunk_kernel, chunk=chunk),
        grid=(B, T // chunk),
        in_specs=[tok] * 6 + [st],
        out_specs=(tok, st),
        out_shape=(jax.ShapeDtypeStruct((B, T, RWKV_W), F32),
                   jax.ShapeDtypeStruct((B, n_pairs, V7X_LANES, V7X_LANES), F32)),
        scratch_shapes=[pltpu.VMEM((n_pairs, V7X_LANES, V7X_LANES), F32)],
        compiler_params=_cparams("parallel", "arbitrary"),
        name="rwkv_chunk_%d" % chunk,
    )(r, k, v, lw, kk, b, _pair_state(S0))
    return o, _unpair_state(s_out)


def _layer_norm(x, g, b):
    mu = jnp.mean(x, axis=-1, keepdims=True)
    var = jnp.mean(jnp.square(x - mu), axis=-1, keepdims=True)
    return (x - mu) * lax.rsqrt(var + LN_EPS) * g + b


def _mix_out_kernel(ga_ref, gr_ref, att_ref, o_ref, bonus_ref, g_ref, x_ref, wout_ref, gng_ref, gnb_ref,
                    ln_g_ref, ln_b_ref, h32_ref, h16_ref):
    ones_bd = _head_ones(V7X_LANES, RWKV_HD)
    o = o_ref[...]
    mean = _segsum(o, ones_bd) * (1.0 / RWKV_HD)
    d = o - mean
    var = _segsum(d * d, ones_bd) * (1.0 / RWKV_HD)
    rw = d * lax.rsqrt(var + RWKV_GN_EPS) * gng_ref[...] + gnb_ref[...]
    rw = (rw + bonus_ref[...]) * g_ref[...]
    mixed_in = jax.nn.sigmoid(ga_ref[...]) * att_ref[...] + jax.nn.sigmoid(gr_ref[...]) * rw
    mixed = _dot(mixed_in.astype(BF16), wout_ref[...])
    h = _layer_norm(DN_ALPHA * x_ref[...] + mixed, ln_g_ref[...], ln_b_ref[...])
    h32_ref[...] = h
    h16_ref[...] = h.astype(BF16)


def _mix_out(gates, att, o, bonus, g, x2d, wout16, gn_g, gn_b, ln_g, ln_b, tm):
    n = x2d.shape[0]
    row = lambda i: (i, 0)
    const = lambda i: (0, 0)
    tok = pl.BlockSpec((tm, D_MODEL), row)
    vec = pl.BlockSpec((1, D_MODEL), const)
    return pl.pallas_call(
        _mix_out_kernel,
        grid=(n // tm,),
        in_specs=[tok, pl.BlockSpec((tm, D_MODEL), lambda i: (i, 1)), tok, tok, tok, tok, tok,
                  pl.BlockSpec((D_MODEL, D_MODEL), const), vec, vec, vec, vec],
        out_specs=(tok, tok),
        out_shape=(jax.ShapeDtypeStruct((n, D_MODEL), F32), jax.ShapeDtypeStruct((n, D_MODEL), BF16)),
        compiler_params=_cparams("parallel"),
        name="mix_out",
    )(gates, gates, att, o, bonus, g, x2d, wout16, gn_g.reshape(1, -1), gn_b.reshape(1, -1),
      ln_g.reshape(1, -1), ln_b.reshape(1, -1))


_STAIR = tuple((p, q) for p in range(PEER_TOPK) for q in range(PEER_TOPK // (p + 1)))
_STAIR_ROWS = -(-len(_STAIR) // V7X_SUBLANES) * V7X_SUBLANES


def _top_ranks(x, k):
    R = x.shape[0]
    row = lax.broadcasted_iota(jnp.int32, x.shape, 0)
    rank = jnp.full(x.shape, float(k), F32)
    vals = []
    for it in range(k):
        m = jnp.max(x, axis=0, keepdims=True)
        idx = jnp.min(jnp.where(x == m, row, R), axis=0, keepdims=True)
        hit = row == idx
        rank = jnp.where(hit, float(it), rank)
        x = jnp.where(hit, -jnp.inf, x)
        vals.append(m)
    return vals, rank


def _peer_select_kernel(h_ref, wq_ref, keys_ref, cut_ref, f1_ref, rank2_ref, e2_ref, cand_scr, sel_scr):
    q = _dot(h_ref[...], wq_ref[...]).astype(BF16)
    K = PEER_TOPK
    for h in range(PEER_HEADS):
        s = []
        for c in range(2):
            blk = (h * 2 + c) * PEER_HALF
            s.append(_dot_nt(keys_ref[h * 2 + c], q[:, blk:blk + PEER_HALF]))
        a1, rank1 = _top_ranks(s[0], K)
        a2, rank2 = _top_ranks(s[1], K)
        cand_scr[...] = jnp.full(cand_scr.shape, -jnp.inf, F32)
        for i, (p, qq) in enumerate(_STAIR):
            cand_scr[i:i + 1, :] = a1[p] + a2[qq]
        cand = cand_scr[...]
        _, crank = _top_ranks(cand, K)
        chosen = crank < float(K)
        z = jnp.sum(jnp.where(chosen, jnp.exp(cand - (a1[0] + a2[0])), 0.0), axis=0, keepdims=True)
        sel_scr[...] = jnp.where(chosen, 1.0, 0.0)
        cut = jnp.zeros(rank1.shape, F32)
        start = 0
        for p in range(K):
            n_p = K // (p + 1)
            cut_p = jnp.sum(sel_scr[start:start + n_p, :], axis=0, keepdims=True)
            cut = jnp.where(rank1 == float(p), cut_p, cut)
            start += n_p
        cut_ref[h] = cut
        f1_ref[h] = jnp.exp(s[0] - a1[0]) * (0.5 / z)
        e2 = jnp.exp(s[1] - a2[0])
        group = rank2_ref.shape[-1]
        for g in range(rank2.shape[1] // group):
            cols = slice(g * group, (g + 1) * group)
            rank2_ref[h, g] = rank2[:, cols].reshape(_KEY_TILES, V7X_SUBLANES, group).astype(BF16)
            e2_ref[h, g] = e2[:, cols].reshape(_KEY_TILES, V7X_SUBLANES, group).astype(BF16)


_KEY_TILES = N_KEYS // V7X_SUBLANES


def _peer_group(n):
    return 2 * V7X_LANES if n % (2 * V7X_LANES) == 0 else V7X_LANES


def _peer_select(h16, wq16, keys16, tp):
    n = h16.shape[0]
    group = _peer_group(tp)
    sel_shape = jax.ShapeDtypeStruct((PEER_HEADS, N_KEYS, n), F32)
    sel_spec = pl.BlockSpec((PEER_HEADS, N_KEYS, tp), lambda i: (0, 0, i))
    tab_shape = jax.ShapeDtypeStruct((PEER_HEADS, n // group, _KEY_TILES, V7X_SUBLANES, group), BF16)
    tab_spec = pl.BlockSpec((PEER_HEADS, tp // group, _KEY_TILES, V7X_SUBLANES, group),
                            lambda i: (0, i, 0, 0, 0))
    return pl.pallas_call(
        _peer_select_kernel,
        grid=(n // tp,),
        in_specs=[pl.BlockSpec((tp, D_MODEL), lambda i: (i, 0)),
                  pl.BlockSpec(wq16.shape, lambda i: (0, 0)),
                  pl.BlockSpec(keys16.shape, lambda i: (0, 0, 0))],
        out_specs=(sel_spec, sel_spec, tab_spec, tab_spec),
        out_shape=(sel_shape, sel_shape, tab_shape, tab_shape),
        scratch_shapes=[pltpu.VMEM((_STAIR_ROWS, tp), F32), pltpu.VMEM((_STAIR_ROWS, tp), F32)],
        compiler_params=_cparams("parallel"),
        name="peer_select",
    )(h16, wq16, keys16)


_PEER_I1_PER_BLOCK = 8
_PEER_I1_PER_CHUNK = 2
_SQRT_HALF = math.sqrt(0.5)


def _peer_dense_kernel(x_ref, u_ref, vt_ref, cut_ref, f1_ref, rank2_ref, e2_ref, h_ref, ln_g_ref, ln_b_ref,
                       y_ref, acc_scr, hid_scr, a_scr):
    e = pl.program_id(1)

    @pl.when(e == 0)
    def _():
        acc_scr[...] = jnp.zeros(acc_scr.shape, F32)
        hid_scr[...] = jnp.zeros(hid_scr.shape, F32)

    group = rank2_ref.shape[-1]
    n_groups = x_ref.shape[0] // group
    tile3 = (_KEY_TILES, V7X_SUBLANES, group)

    def stages(cur, prev):
        hid = _dot_nt(u_ref[...], x_ref[...])
        for tc in range(n_groups):
            hid_scr[cur, tc] = hid[:, tc * group:(tc + 1) * group]
        for ch in range(_PEER_I1_PER_BLOCK // _PEER_I1_PER_CHUNK):
            for il in range(ch * _PEER_I1_PER_CHUNK, (ch + 1) * _PEER_I1_PER_CHUNK):
                rows = slice(il * N_KEYS, (il + 1) * N_KEYS)
                for tc in range(n_groups):
                    cols = slice(tc * group, (tc + 1) * group)
                    w = jnp.zeros(tile3, BF16)
                    for h in range(PEER_HEADS):
                        cutb = jnp.broadcast_to(cut_ref[h, il:il + 1, cols], tile3[1:]).astype(BF16)
                        f1b = jnp.broadcast_to(f1_ref[h, il:il + 1, cols], tile3[1:]).astype(BF16)
                        e2 = e2_ref[h, tc]
                        w = w + jnp.where(rank2_ref[h, tc] < cutb[None], e2 * f1b[None], jnp.zeros_like(e2))
                    hb = hid_scr[prev, tc, rows, :]
                    act = (hb * (1.0 + lax.erf(hb * _SQRT_HALF))).reshape(tile3).astype(BF16)
                    a_scr[rows, cols] = (w * act).reshape(N_KEYS, group)
            crow = slice(ch * _PEER_I1_PER_CHUNK * N_KEYS, (ch + 1) * _PEER_I1_PER_CHUNK * N_KEYS)
            acc_scr[...] += _dot(vt_ref[:, crow], a_scr[crow, :])

    for parity in range(2):
        @pl.when(e % 2 == parity)
        def _(parity=parity):
            stages(parity, 1 - parity)

    @pl.when(e == pl.num_programs(1) - 1)
    def _():
        ff = acc_scr[...].T
        y_ref[...] = _layer_norm(DN_ALPHA * h_ref[...] + ff, ln_g_ref[...], ln_b_ref[...])


def _peer_dense(h32, h16, u16, vt16, sel, ln_g, ln_b, tt):
    n = h32.shape[0]
    cut, f1, rank2, e2 = sel
    group = rank2.shape[-1]
    assert tt % group == 0
    eb = _PEER_I1_PER_BLOCK * N_KEYS
    n_e = N_EXPERTS // eb
    tok = lambda i, e: (i, 0)
    const = lambda i, e: (0, 0)
    blk = lambda e, lag: jnp.clip(e - lag, 0, n_e - 1)
    tab_spec = pl.BlockSpec((PEER_HEADS, tt // group, _KEY_TILES, V7X_SUBLANES, group), lambda i, e: (0, i, 0, 0, 0))
    return pl.pallas_call(
        _peer_dense_kernel,
        grid=(n // tt, n_e + 1),
        in_specs=[pl.BlockSpec((tt, D_MODEL), tok),
                  pl.BlockSpec((eb, D_MODEL), lambda i, e: (blk(e, 0), 0)),
                  pl.BlockSpec((D_MODEL, eb), lambda i, e: (0, blk(e, 1))),
                  pl.BlockSpec((PEER_HEADS, _PEER_I1_PER_BLOCK, tt), lambda i, e: (0, blk(e, 1), i)),
                  pl.BlockSpec((PEER_HEADS, _PEER_I1_PER_BLOCK, tt), lambda i, e: (0, blk(e, 1), i)),
                  tab_spec, tab_spec,
                  pl.BlockSpec((tt, D_MODEL), tok),
                  pl.BlockSpec((1, D_MODEL), const), pl.BlockSpec((1, D_MODEL), const)],
        out_specs=pl.BlockSpec((tt, D_MODEL), tok),
        out_shape=jax.ShapeDtypeStruct((n, D_MODEL), F32),
        scratch_shapes=[pltpu.VMEM((D_MODEL, tt), F32), pltpu.VMEM((2, tt // group, eb, group), F32),
                        pltpu.VMEM((eb, tt), BF16)],
        compiler_params=_cparams("parallel", "arbitrary"),
        name="peer_dense",
    )(h16, u16, vt16, cut, f1, rank2, e2, h32, ln_g.reshape(1, -1), ln_b.reshape(1, -1))


_TM_PROMPT = 256
_TQ_PROMPT = 512
_RWKV_CHUNK = 64
_RWKV_CHUNK_SAMPLE = 16
_PAGES_PER_STEP = 8
_PEER_TP = 256
_PEER_TT = 512


def _layer_tail(gates, att2, o2, bonus2, g2, x2, lw, tm, tp, tt):
    h32, h16 = _mix_out(gates, att2, o2, bonus2, g2, x2, lw["wout16"], lw["rwkv_ln_g"], lw["rwkv_ln_b"],
                        lw["ln1_g"], lw["ln1_b"], tm)
    sel = _peer_select(h16, lw["wq16"], lw["keys16"], tp)
    return _peer_dense(h32, h16, lw["u16"], lw["vt16"], sel, lw["ln2_g"], lw["ln2_b"], tt)


def _prompt_layer(x, lw, rel_bias):
    B, T, _ = x.shape
    n = B * T
    x2 = x.reshape(n, D_MODEL)
    q16, k32, k16, v32, v16, gates, p = _in_proj(x2, lw["w_in16"], _TM_PROMPT)
    sh = lambda t: t.reshape(B, T, -1)
    vt16 = jnp.swapaxes(sh(v16), 1, 2)
    att = _prompt_attention(sh(q16), sh(k16), vt16, rel_bias, lw["lam_qk"], lw["attn_norm_g"], _TQ_PROMPT)
    p3 = sh(p)
    shift0 = jnp.zeros((B, RWKV_PROJ), F32)
    r, k, v, lgw, kk, b, g, bonus = _rwkv_prep_prompt(p3, shift0, lw["rwkv_params"], _TM_PROMPT)
    S0 = jnp.zeros((B, RWKV_HEADS, RWKV_HD, RWKV_HD), F32)
    o, S_new = _rwkv_chunked(r, k, v, lgw, kk, b, S0, _RWKV_CHUNK)
    fl = lambda t: t.reshape(n, -1)
    y = _layer_tail(gates, fl(att), fl(o), fl(bonus), fl(g), x2, lw, _TM_PROMPT, _PEER_TP, _PEER_TT)
    return (y.reshape(B, T, D_MODEL), k32.reshape(B, T, ATT_HEADS, 2 * ATT_HD), v32.reshape(B, T, ATT_HEADS, ATT_VD),
            S_new, p3[:, -1, :])


def _sample_layer(x, lw, rel_bias, cache_k, cache_v, layer, page_table, shift0, S0):
    DB, S, _ = x.shape
    n = DB * S
    x2 = x.reshape(n, D_MODEL)
    q16, k32, k16, v32, v16, gates, p = _in_proj(x2, lw["w_in16"], n)
    sh = lambda t: t.reshape(DB, S, -1)
    att = _sample_attention(sh(q16), sh(k32), sh(v32), cache_k, cache_v, layer, page_table, rel_bias,
                            lw["lam_qk"], lw["attn_norm_g"], _PAGES_PER_STEP)
    p3 = sh(p)
    prev = jnp.concatenate([shift0[:, None, :], p3[:, :-1, :]], axis=1).reshape(n, RWKV_PROJ)
    r, k, v, lgw, kk, b, g, bonus = _rwkv_prep_sample(p, prev, lw["rwkv_params"])
    pad = lambda t: jnp.pad(sh(t), ((0, 0), (0, _RWKV_CHUNK_SAMPLE - S), (0, 0)))
    o, S_new = _rwkv_chunked(pad(r), pad(k), pad(v), pad(lgw), pad(kk), pad(b), S0, _RWKV_CHUNK_SAMPLE)
    o2 = o[:, :S, :].reshape(n, RWKV_W)
    y = _layer_tail(gates, att.reshape(n, ATT_V), o2, bonus, g, x2, lw, n, n, n)
    return (y.reshape(DB, S, D_MODEL), k32.reshape(DB, S, ATT_HEADS, 2 * ATT_HD), v32.reshape(DB, S, ATT_HEADS, ATT_VD),
            S_new, p3[:, -1, :])


def kernel(x_prompt, x_sample, cache_k, cache_v, state_wkv, state_shift, page_table, w_in, w_out, lam_qk, attn_norm_g, shift_mu, rwkv_w0, rwkv_w2, rwkv_a0, rwkv_a2, rwkv_g2, rwkv_kk, rwkv_ka, rwkv_rk, rwkv_ln_g, rwkv_ln_b, ln1_g, ln1_b, ln2_g, ln2_b, peer_wq, peer_keys, peer_u, peer_v, rel_bias):
    assert w_in.shape[0] == DEPTH == 1
    yp, ys = x_prompt, x_sample
    outs = [[] for _ in range(8)]
    for l in range(DEPTH):
        lw = {
            "w_in16": w_in[l].astype(BF16), "wout16": w_out[l].astype(BF16),
            "lam_qk": lam_qk[l], "attn_norm_g": attn_norm_g[l],
            "rwkv_params": _rwkv_param_list(shift_mu[l], rwkv_w0[l], rwkv_w2[l], rwkv_a0[l], rwkv_a2[l],
                                            rwkv_g2[l], rwkv_kk[l], rwkv_ka[l], rwkv_rk[l]),
            "rwkv_ln_g": rwkv_ln_g[l], "rwkv_ln_b": rwkv_ln_b[l],
            "ln1_g": ln1_g[l], "ln1_b": ln1_b[l], "ln2_g": ln2_g[l], "ln2_b": ln2_b[l],
            "wq16": peer_wq[l].astype(BF16),
            "keys16": peer_keys[l].reshape(PEER_HEADS * 2, N_KEYS, PEER_HALF).astype(BF16),
            "u16": peer_u[l].astype(BF16), "vt16": peer_v[l].T.astype(BF16),
        }
        yp, kp, vp, Sp, shp = _prompt_layer(yp, lw, rel_bias)
        ys, ksm, vsm, Ss, shs = _sample_layer(ys, lw, rel_bias, cache_k, cache_v, l, page_table,
                                              state_shift[l], state_wkv[l])
        for lst, val in zip(outs, (kp, vp, ksm, vsm, Sp, Ss, shp, shs)):
            lst.append(val)
    stacked = tuple(jnp.stack(lst, axis=0) for lst in outs)
    return (yp, ys) + stacked
```

```python
import functools
import math

import jax
import jax.numpy as jnp
from jax import lax
from jax.experimental import pallas as pl
from jax.experimental.pallas import tpu as pltpu

D_MODEL = 1024
PAGE_SIZE = 128
ATT_HD = 64
ATT_HEADS = D_MODEL // (2 * ATT_HD)
ATT_VD = 2 * ATT_HD
ATT_QK = ATT_HEADS * 2 * ATT_HD
ATT_V = ATT_HEADS * ATT_VD
N_BUCKETS = 32
MAX_DISTANCE = 128
ATT_EPS = 1e-5
RWKV_HD = 64
RWKV_HEADS = D_MODEL // RWKV_HD
RWKV_W = RWKV_HEADS * RWKV_HD
DECAY_LORA = 64
AAA_LORA = 64
GATE_LORA = 128
RWKV_PROJ = 3 * RWKV_W + DECAY_LORA + AAA_LORA + GATE_LORA
DECAY_SCALE = 0.606531
RWKV_GN_EPS = 64e-5
O_K = ATT_QK
O_V = O_K + ATT_QK
O_GA = O_V + ATT_V
O_GR = O_GA + D_MODEL
O_RW = O_GR + D_MODEL
N_IN = O_RW + RWKV_PROJ
PEER_HEADS = 8
N_KEYS = 128
N_EXPERTS = N_KEYS * N_KEYS
PEER_QDIM = 256
PEER_HALF = PEER_QDIM // 2
PEER_TOPK = 16
DEPTH = 1
DN_ALPHA = (2.0 * DEPTH) ** 0.25
LN_EPS = 1e-5
LAMBDA_INIT = 0.8 - 0.6 * math.exp(-0.3 * 0)

V7X_LANES = 128
V7X_SUBLANES = 8
V7X_VMEM_LIMIT_BYTES = 56 * 1024 * 1024

NEG_INF = -1e30
LOG2E = math.log2(math.e)
BF16 = jnp.bfloat16
F32 = jnp.float32

NT_DIMS = (((1,), (1,)), ((), ()))
TN_DIMS = (((0,), (0,)), ((), ()))


def _cparams(*sem, flags=None):
    return pltpu.CompilerParams(dimension_semantics=sem, vmem_limit_bytes=V7X_VMEM_LIMIT_BYTES, flags=flags)


def _dot(a, b):
    return jnp.dot(a, b, preferred_element_type=F32)


def _dot_nt(a, b):
    return lax.dot_general(a, b, NT_DIMS, preferred_element_type=F32)


def _dot_tn(a, b):
    return lax.dot_general(a, b, TN_DIMS, preferred_element_type=F32)


def _split_dot(x, w_bf16):
    hi = x.astype(BF16)
    lo = (x - hi.astype(F32)).astype(BF16)
    return _dot(hi, w_bf16) + _dot(lo, w_bf16)


_P_CHUNKS = ((0, 1024), (1024, 2048), (2048, 3072), (3072, RWKV_PROJ))


def _in_proj_kernel(x_ref, w_ref, q16_ref, k32_ref, k16_ref, v32_ref, v16_ref, gates_ref, p_ref):
    x = x_ref[...].astype(BF16)

    def mm(lo, hi):
        return _dot(x, w_ref[:, lo:hi])

    q16_ref[...] = (mm(0, O_K) * (ATT_HD ** -0.5 * LOG2E)).astype(BF16)
    k = mm(O_K, O_V)
    k32_ref[...] = k
    k16_ref[...] = k.astype(BF16)
    v = mm(O_V, O_GA)
    v32_ref[...] = v
    v16_ref[...] = v.astype(BF16)
    gates_ref[:, 0:D_MODEL] = mm(O_GA, O_GR)
    gates_ref[:, D_MODEL:2 * D_MODEL] = mm(O_GR, O_RW)
    for lo, hi in _P_CHUNKS:
        p_ref[:, lo:hi] = mm(O_RW + lo, O_RW + hi)


def _in_proj(x2d, w_bf16, tm):
    n = x2d.shape[0]
    row = lambda i: (i, 0)
    full = lambda i: (0, 0)
    out_shape = (
        jax.ShapeDtypeStruct((n, ATT_QK), BF16),
        jax.ShapeDtypeStruct((n, ATT_QK), F32),
        jax.ShapeDtypeStruct((n, ATT_QK), BF16),
        jax.ShapeDtypeStruct((n, ATT_V), F32),
        jax.ShapeDtypeStruct((n, ATT_V), BF16),
        jax.ShapeDtypeStruct((n, 2 * D_MODEL), F32),
        jax.ShapeDtypeStruct((n, RWKV_PROJ), F32),
    )
    out_specs = tuple(pl.BlockSpec((tm, s.shape[1]), row) for s in out_shape)
    return pl.pallas_call(
        _in_proj_kernel,
        grid=(n // tm,),
        in_specs=[pl.BlockSpec((tm, D_MODEL), row),
                  pl.BlockSpec((D_MODEL, N_IN), full, pipeline_mode=pl.Buffered(1))],
        out_specs=out_specs,
        out_shape=out_shape,
        compiler_params=_cparams("parallel"),
        name="in_proj",
    )(x2d, w_bf16)


def _t5_bucket(rel):
    n = jnp.maximum(rel, 0)
    max_exact = N_BUCKETS // 2
    nf = jnp.maximum(n, 1).astype(F32)
    large = max_exact + (jnp.log(nf / max_exact) / math.log(MAX_DISTANCE / max_exact)
                         * (N_BUCKETS - max_exact)).astype(jnp.int32)
    large = jnp.minimum(large, N_BUCKETS - 1)
    return jnp.where(n < max_exact, n, large)


def _shifted_bias(rel_bias, dist):
    far = lax.dynamic_index_in_dim(rel_bias, _t5_bucket(jnp.int32(2 * MAX_DISTANCE)), 0, keepdims=False)
    table = (rel_bias - far[None, :]) * LOG2E
    bucket = _t5_bucket(dist)[None]
    expand = (slice(None),) + (None,) * dist.ndim
    out = jnp.zeros((rel_bias.shape[1],) + dist.shape, F32)
    for b in range(N_BUCKETS):
        out = jnp.where(bucket == b, table[b][expand], out)
    return out


def _per_row_head(tiles, row_h):
    out = jnp.zeros(tiles.shape[1:], F32)
    for h in range(tiles.shape[0]):
        out = jnp.where(row_h[:, None] == h, tiles[h], out)
    return out


def _lambda_value(lq):
    s01 = jnp.sum(lq[0:1] * lq[1:2], axis=-1, keepdims=True)
    s23 = jnp.sum(lq[2:3] * lq[3:4], axis=-1, keepdims=True)
    return jnp.exp(s01) - jnp.exp(s23) + LAMBDA_INIT


def _rms_scale(o):
    return o * lax.rsqrt(jnp.mean(o * o, axis=-1, keepdims=True) + ATT_EPS)


def _prompt_attn_kernel(q_ref, k_ref, vt_ref, bias_ref, lam_ref, g_ref, o_ref, m_ref, l_ref, acc_ref,
                        sa_ref, sb_ref, *, tq):
    i = pl.program_id(2)
    tk = tq // 2
    lane = lax.broadcasted_iota(jnp.int32, (tq, 2 * ATT_HD), 1)
    q = q_ref[0]
    zero = jnp.zeros_like(q)
    qm = (jnp.where(lane < ATT_HD, q, zero), jnp.where(lane >= ATT_HD, q, zero))

    m_ref[...] = jnp.full(m_ref.shape, NEG_INF, F32)
    l_ref[...] = jnp.zeros(l_ref.shape, F32)
    acc_ref[...] = jnp.zeros(acc_ref.shape, F32)

    def scores(b, s_ref):
        kb = k_ref[0, pl.ds(pl.multiple_of(b * tk, tk), tk), :]
        for c in range(2):
            s_ref[c] = _dot_nt(kb, qm[c])

    def consume(b, s_ref, add_t):
        vt = vt_ref[0, :, pl.ds(pl.multiple_of(b * tk, tk), tk)]
        p16, alphas = [], []
        for c in range(2):
            sc = s_ref[c] if add_t is None else s_ref[c] + add_t
            m_old = m_ref[c]
            m_new = jnp.maximum(m_old, jnp.max(sc, axis=0, keepdims=True))
            alpha = jnp.exp2(m_old - m_new)
            p = jnp.exp2(sc - m_new)
            l_ref[c] = alpha * l_ref[c] + jnp.sum(p, axis=0, keepdims=True)
            m_ref[c] = m_new
            p16.append(p.astype(BF16))
            alphas.append(alpha)
        for c in range(2):
            acc_ref[c] = alphas[c] * acc_ref[c] + _dot(vt, p16[c])

    scores(0, sa_ref)

    def far_body(p, carry):
        scores(2 * p + 1, sb_ref)
        consume(2 * p, sa_ref, None)
        scores(2 * p + 2, sa_ref)
        consume(2 * p + 1, sb_ref, None)
        return carry

    lax.fori_loop(0, jnp.maximum(i - 1, 0), far_body, 0)

    @pl.when(i > 0)
    def _():
        scores(2 * i - 1, sb_ref)
        consume(2 * i - 2, sa_ref, None)
        scores(2 * i, sa_ref)
        consume(2 * i - 1, sb_ref, bias_ref[0, 0])

    scores(2 * i + 1, sb_ref)
    consume(2 * i, sa_ref, bias_ref[0, 1])
    consume(2 * i + 1, sb_ref, bias_ref[0, 2])

    lam = _lambda_value(lam_ref[...])
    ot = acc_ref[0] / l_ref[0] - lam * (acc_ref[1] / l_ref[1])
    ot = ot * lax.rsqrt(jnp.mean(ot * ot, axis=0, keepdims=True) + ATT_EPS)
    o_ref[0] = ot.T * (g_ref[...] * (1.0 - LAMBDA_INIT))


def _prompt_attention(q16, k16, vt16, rel_bias, lam_qk, attn_norm_g, tq):
    B, T, _ = q16.shape
    tk = tq // 2
    assert tk >= MAX_DISTANCE
    diff_t = jnp.arange(tq, dtype=jnp.int32)[None, :] - jnp.arange(tk, dtype=jnp.int32)[:, None]
    tiles = []
    for off in (tk, 0, -tk):
        dist = diff_t + off
        tiles.append(jnp.where(dist >= 0, _shifted_bias(rel_bias, dist), NEG_INF))
    bias = jnp.stack(tiles, axis=1).astype(F32)
    kern = functools.partial(_prompt_attn_kernel, tq=tq)
    return pl.pallas_call(
        kern,
        grid=(B, ATT_HEADS, T // tq),
        in_specs=[
            pl.BlockSpec((1, tq, 2 * ATT_HD), lambda b, h, i: (b, i, h)),
            pl.BlockSpec((1, T, 2 * ATT_HD), lambda b, h, i: (b, 0, h)),
            pl.BlockSpec((1, ATT_VD, T), lambda b, h, i: (b, h, 0)),
            pl.BlockSpec((1, 3, tk, tq), lambda b, h, i: (h, 0, 0, 0)),
            pl.BlockSpec((4, ATT_HD), lambda b, h, i: (0, 0)),
            pl.BlockSpec((1, ATT_VD), lambda b, h, i: (0, h)),
        ],
        out_specs=pl.BlockSpec((1, tq, ATT_VD), lambda b, h, i: (b, i, h)),
        out_shape=jax.ShapeDtypeStruct((B, T, ATT_V), F32),
        scratch_shapes=[pltpu.VMEM((2, 1, tq), F32), pltpu.VMEM((2, 1, tq), F32),
                        pltpu.VMEM((2, ATT_VD, tq), F32),
                        pltpu.VMEM((2, tk, tq), F32), pltpu.VMEM((2, tk, tq), F32)],
        compiler_params=_cparams("parallel", "parallel", "arbitrary"),
        name="prompt_attention",
    )(q16, k16, vt16, bias, lam_qk, attn_norm_g.reshape(1, ATT_V))


def _sample_attn_kernel(pt_ref, q_ref, plain_ref, last_ref, new_ref, knew_ref, vnew_ref, lam_ref, g_ref,
                        *rest, pages_per_step):
    k_refs = rest[:pages_per_step]
    v_refs = rest[pages_per_step:2 * pages_per_step]
    o_ref, m_ref, l_ref, acc_ref = rest[2 * pages_per_step:]
    j = pl.program_id(1)
    last = pl.num_programs(1) - 1
    q = q_ref[0]

    @pl.when(j == 0)
    def _():
        m_ref[...] = jnp.full(m_ref.shape, NEG_INF, F32)
        l_ref[...] = jnp.zeros(l_ref.shape, F32)
        acc_ref[...] = jnp.zeros(acc_ref.shape, F32)

    def update(s, v2d):
        m_old = m_ref[...]
        m_new = jnp.maximum(m_old, jnp.max(s, axis=-1, keepdims=True))
        alpha = jnp.exp2(m_old - m_new)
        p = jnp.exp2(s - m_new)
        l_ref[...] = alpha * l_ref[...] + jnp.sum(p, axis=-1, keepdims=True)
        acc_ref[...] = alpha * acc_ref[...] + _dot(p.astype(BF16), v2d)
        m_ref[...] = m_new

    rows2d = PAGE_SIZE * ATT_HEADS
    k_all = jnp.concatenate([r[...].reshape(rows2d, 2 * ATT_HD).astype(BF16) for r in k_refs], axis=0)
    v_all = jnp.concatenate([r[...].reshape(rows2d, ATT_VD).astype(BF16) for r in v_refs], axis=0)
    add = jnp.concatenate([plain_ref[...]] * (pages_per_step - 1) + [last_ref[0]], axis=1)
    update(_dot_nt(q, k_all) + add, v_all)

    @pl.when(j == last)
    def _():
        n_new = knew_ref.shape[1]
        k2d = knew_ref[0].reshape(n_new * ATT_HEADS, 2 * ATT_HD).astype(BF16)
        v2d = vnew_ref[0].reshape(n_new * ATT_HEADS, ATT_VD).astype(BF16)
        update(_dot_nt(q, k2d) + new_ref[...], v2d)
        half = acc_ref.shape[0] // 2
        o = acc_ref[...] / l_ref[...]
        lam = _lambda_value(lam_ref[...])
        diff = o[0:half] - lam * o[half:2 * half]
        o_ref[0] = _rms_scale(diff) * (g_ref[...] * (1.0 - LAMBDA_INIT))


def _sample_attention(q16, k32, v32, cache_k, cache_v, layer, page_table, rel_bias, lam_qk, attn_norm_g,
                      pages_per_step):
    DB, S, _ = q16.shape
    n_pages = page_table.shape[1]
    past = n_pages * PAGE_SIZE
    H = ATT_HEADS
    rows = 2 * S * H
    q5 = q16.reshape(DB, S, H, 2, ATT_HD)
    qrows = jnp.stack([jnp.concatenate([q5[:, :, :, 0], jnp.zeros_like(q5[:, :, :, 0])], axis=-1),
                       jnp.concatenate([jnp.zeros_like(q5[:, :, :, 1]), q5[:, :, :, 1]], axis=-1)], axis=1)
    qrows = qrows.reshape(DB, rows, 2 * ATT_HD)
    row_s = (jnp.arange(rows, dtype=jnp.int32) // H) % S
    row_h = jnp.arange(rows, dtype=jnp.int32) % H
    col_t = jnp.arange(PAGE_SIZE * H, dtype=jnp.int32) // H
    col_h = jnp.arange(PAGE_SIZE * H, dtype=jnp.int32) % H
    same_head = row_h[:, None] == col_h[None, :]
    plain = jnp.where(same_head, 0.0, NEG_INF).astype(F32)
    dist_last = PAGE_SIZE + row_s[:, None] - col_t[None, :]
    b_last = _per_row_head(_shifted_bias(rel_bias, dist_last), row_h)
    last_tile = jnp.where(same_head, b_last, NEG_INF).astype(F32)
    tiles = jnp.stack([plain, last_tile], axis=0)
    ncol_t = jnp.arange(S * H, dtype=jnp.int32) // H
    ncol_h = jnp.arange(S * H, dtype=jnp.int32) % H
    dist_new = row_s[:, None] - ncol_t[None, :]
    b_new = _per_row_head(_shifted_bias(rel_bias, dist_new), row_h)
    ok_new = (row_h[:, None] == ncol_h[None, :]) & (dist_new >= 0)
    new_tile = jnp.where(ok_new, b_new, NEG_INF).astype(F32)
    g_tile = jnp.tile(attn_norm_g.reshape(H, ATT_VD), (S, 1))

    assert n_pages % pages_per_step == 0
    n_steps = n_pages // pages_per_step
    kern = functools.partial(_sample_attn_kernel, pages_per_step=pages_per_step)

    def page_spec(g):
        return pl.BlockSpec((None, None, PAGE_SIZE, H, 2 * ATT_HD),
                            lambda b, j, pt, g=g: (layer, pt[b, j * pages_per_step + g], 0, 0, 0))

    grid_spec = pltpu.PrefetchScalarGridSpec(
        num_scalar_prefetch=1,
        grid=(DB, n_steps),
        in_specs=[
            pl.BlockSpec((1, rows, 2 * ATT_HD), lambda b, j, pt: (b, 0, 0)),
            pl.BlockSpec((None, rows, PAGE_SIZE * H), lambda b, j, pt: (0, 0, 0)),
            pl.BlockSpec((1, rows, PAGE_SIZE * H), lambda b, j, pt: (jnp.where(j == n_steps - 1, 1, 0), 0, 0)),
            pl.BlockSpec((rows, S * H), lambda b, j, pt: (0, 0)),
            pl.BlockSpec((1, S, H, 2 * ATT_HD), lambda b, j, pt: (b, 0, 0, 0)),
            pl.BlockSpec((1, S, H, ATT_VD), lambda b, j, pt: (b, 0, 0, 0)),
            pl.BlockSpec((4, ATT_HD), lambda b, j, pt: (0, 0)),
            pl.BlockSpec((S * H, ATT_VD), lambda b, j, pt: (0, 0)),
        ] + [page_spec(g) for g in range(pages_per_step)] + [page_spec(g) for g in range(pages_per_step)],
        out_specs=pl.BlockSpec((1, S * H, ATT_VD), lambda b, j, pt: (b, 0, 0)),
        scratch_shapes=[pltpu.VMEM((rows, 1), F32), pltpu.VMEM((rows, 1), F32), pltpu.VMEM((rows, ATT_VD), F32)],
    )
    out = pl.pallas_call(
        kern,
        grid_spec=grid_spec,
        out_shape=jax.ShapeDtypeStruct((DB, S * H, ATT_VD), F32),
        compiler_params=_cparams("parallel", "arbitrary"),
        name="sample_attention",
    )(page_table, qrows, tiles, tiles, new_tile,
      k32.reshape(DB, S, H, 2 * ATT_HD), v32.reshape(DB, S, H, ATT_VD), lam_qk, g_tile,
      *([cache_k] * pages_per_step), *([cache_v] * pages_per_step))
    return out.reshape(DB, S, ATT_V)


def _head_ones(n, seg):
    r = lax.broadcasted_iota(jnp.int32, (n, n), 0) // seg
    c = lax.broadcasted_iota(jnp.int32, (n, n), 1) // seg
    return jnp.where(r == c, 1.0, 0.0).astype(BF16)


def _segsum(x, ones_bd):
    outs = []
    for g in range(x.shape[1] // V7X_LANES):
        outs.append(_split_dot(x[:, g * V7X_LANES:(g + 1) * V7X_LANES], ones_bd))
    return jnp.concatenate(outs, axis=1)


def _rwkv_prep_core(p, prev, mu_ref, w0_ref, w2_ref, a0_ref, a2_ref, g2_ref, kks_ref, ka_ref, rk_ref, outs):
    r_ref, k_ref, v_ref, lw_ref, kk_ref, b_ref, g_ref, bonus_ref = outs
    W = RWKV_W
    xs = p + (prev - p) * mu_ref[...]
    r = xs[:, 0:W]
    k = xs[:, W:2 * W]
    v = xs[:, 2 * W:3 * W]
    wa = xs[:, 3 * W:3 * W + DECAY_LORA + AAA_LORA]
    gd = xs[:, 3 * W + DECAY_LORA + AAA_LORA:RWKV_PROJ]
    logw = -DECAY_SCALE * jax.nn.sigmoid(w0_ref[...] + _dot(jnp.tanh(wa).astype(BF16), w2_ref[...]))
    a = jax.nn.sigmoid(a0_ref[...] + _dot(wa.astype(BF16), a2_ref[...]))
    g = _dot(jax.nn.sigmoid(gd).astype(BF16), g2_ref[...])
    ones_bd = _head_ones(V7X_LANES, RWKV_HD)
    kk = k * kks_ref[...]
    kk = kk / jnp.maximum(jnp.sqrt(_segsum(kk * kk, ones_bd)), 1e-12)
    k = k * (1.0 + (a - 1.0) * ka_ref[...])
    r_ref[...] = r
    k_ref[...] = k
    v_ref[...] = v
    lw_ref[...] = logw
    kk_ref[...] = kk
    b_ref[...] = kk * a
    g_ref[...] = g
    bonus_ref[...] = _segsum(r * k * rk_ref[...], ones_bd) * v


def _rwkv_prep_sample_kernel(p_ref, prev_ref, *rest):
    params, outs = rest[:9], rest[9:]
    _rwkv_prep_core(p_ref[...], prev_ref[...], *params, outs)


def _rwkv_param_list(shift_mu, rwkv_w0, rwkv_w2, rwkv_a0, rwkv_a2, rwkv_g2, rwkv_kk, rwkv_ka, rwkv_rk):
    zeros = jnp.zeros((AAA_LORA, RWKV_W), F32)
    w2p = jnp.concatenate([rwkv_w2, zeros], axis=0).astype(BF16)
    a2p = jnp.concatenate([jnp.zeros((DECAY_LORA, RWKV_W), F32), rwkv_a2], axis=0).astype(BF16)
    row = lambda t: t.reshape(1, -1)
    return [row(shift_mu), row(rwkv_w0), w2p, row(rwkv_a0), a2p, rwkv_g2.astype(BF16),
            row(rwkv_kk), row(rwkv_ka), row(rwkv_rk)]


def _rwkv_prep_sample(p2, prev2, params):
    n = p2.shape[0]
    const = lambda i: (0, 0)
    in_specs = [pl.BlockSpec((n, RWKV_PROJ), const), pl.BlockSpec((n, RWKV_PROJ), const)] + [
        pl.BlockSpec(t.shape, const) for t in params]
    out_shape = tuple(jax.ShapeDtypeStruct((n, RWKV_W), F32) for _ in range(8))
    out_specs = tuple(pl.BlockSpec((n, RWKV_W), const) for _ in range(8))
    return pl.pallas_call(
        _rwkv_prep_sample_kernel,
        grid=(1,),
        in_specs=in_specs, out_specs=out_specs, out_shape=out_shape,
        compiler_params=_cparams("arbitrary"),
        name="rwkv_prep_sample",
    )(p2, prev2, *params)


def _rwkv_chunk_kernel(r_ref, k_ref, v_ref, lw_ref, kk_ref, b_ref, s0_ref, o_ref, sout_ref, s_scr, *, chunk):
    _rwkv_chunk_core(r_ref[0], k_ref[0], v_ref[0], lw_ref[0], kk_ref[0], b_ref[0], s0_ref, o_ref, sout_ref, s_scr,
                     chunk)


def _rwkv_fused_kernel(p_ref, tail_ref, shift_ref, *rest, chunk):
    params, rest = rest[:9], rest[9:]
    s0_ref, o_ref, g_ref, bonus_ref, sout_ref, s_scr = rest[:6]
    r_s, k_s, v_s, lw_s, kk_s, b_s = rest[6:]
    c_idx = pl.program_id(1)
    p = p_ref[0]
    first = jnp.where(c_idx == 0, shift_ref[0], tail_ref[0, V7X_SUBLANES - 1:V7X_SUBLANES, :])
    row = lax.broadcasted_iota(jnp.int32, p.shape, 0)
    prev = jnp.where(row == 0, first, pltpu.roll(p, 1, 0))
    _rwkv_prep_core(p, prev, *params, [r_s, k_s, v_s, lw_s, kk_s, b_s, g_ref.at[0], bonus_ref.at[0]])
    _rwkv_chunk_core(r_s[...], k_s[...], v_s[...], lw_s[...], kk_s[...], b_s[...], s0_ref, o_ref, sout_ref, s_scr,
                     chunk)


def _rwkv_chunk_core(r_in, k_in, v_in, lw, kk_in, b_in, s0_ref, o_ref, sout_ref, s_scr, chunk):
    C = chunk
    c_idx = pl.program_id(1)
    n_pairs = RWKV_HEADS // 2
    L = V7X_LANES

    @pl.when(c_idx == 0)
    def _():
        s_scr[...] = s0_ref[0]

    ti = lax.broadcasted_iota(jnp.int32, (C, C), 0)
    si = lax.broadcasted_iota(jnp.int32, (C, C), 1)
    tri = jnp.where(si <= ti, 1.0, 0.0).astype(BF16)
    hi = lw.astype(BF16)
    rem = lw - hi.astype(F32)
    mid = rem.astype(BF16)
    lo = (rem - mid.astype(F32)).astype(BF16)
    cs = _dot(tri, hi) + _dot(tri, mid) + _dot(tri, lo)
    g = jnp.exp(cs)
    gprev = jnp.exp(cs - lw)
    ginv = jnp.exp(-cs)
    kap_all = kk_in * gprev
    rt_all = r_in * g
    kt_all = k_in * ginv
    bt_all = b_in * ginv
    v_all = v_in
    g_last = g[C - 1:C, :]

    lane = lax.broadcasted_iota(jnp.int32, (C, L), 1)
    in_h0 = lane < RWKV_HD

    def stack(x):
        z = jnp.zeros_like(x)
        return jnp.concatenate([jnp.where(in_h0, x, z), jnp.where(in_h0, z, x)], axis=0)

    rr = lax.broadcasted_iota(jnp.int32, (2 * C, 2 * C), 0)
    cc = lax.broadcasted_iota(jnp.int32, (2 * C, 2 * C), 1)
    same = (rr // C) == (cc // C)
    strict = same & (cc < rr)
    incl = same & (cc <= rr)
    eye = jnp.where(rr == cc, 1.0, 0.0).astype(F32)
    n_sq = max(int(math.ceil(math.log2(C))) - 1, 0)

    prs = range(n_pairs)
    sls = [slice(pr * L, (pr + 1) * L) for pr in prs]
    vs16 = [stack(v_all[:, sl]).astype(BF16) for sl in sls]
    lhs = [jnp.concatenate([stack(kap_all[:, sl]).astype(BF16), stack(rt_all[:, sl]).astype(BF16)], axis=0)
           for sl in sls]
    rhs = [jnp.concatenate([stack(kt_all[:, sl]).astype(BF16), stack(bt_all[:, sl]).astype(BF16)], axis=0)
           for sl in sls]
    A = [_dot_nt(lhs[pr], rhs[pr]) for pr in prs]
    S = [s_scr[pr] for pr in prs]
    KS = [_dot_nt(lhs[pr], S[pr].astype(BF16)) for pr in prs]
    zero = jnp.zeros((2 * C, 2 * C), F32)
    Lk16 = [jnp.where(strict, A[pr][0:2 * C, 0:2 * C], zero).astype(BF16) for pr in prs]
    Lb = [jnp.where(strict, A[pr][0:2 * C, 2 * C:4 * C], zero) for pr in prs]
    MkMb = [jnp.concatenate([jnp.where(incl, A[pr][2 * C:4 * C, 0:2 * C], zero),
                             -jnp.where(incl, A[pr][2 * C:4 * C, 2 * C:4 * C], zero)], axis=1).astype(BF16)
            for pr in prs]
    LkV = [_dot(Lk16[pr], vs16[pr]) for pr in prs]
    X = [eye - Lb[pr] for pr in prs]
    P16 = [Lb[pr].astype(BF16) for pr in prs]
    for _ in range(n_sq):
        P16 = [_dot(P16[pr], P16[pr]).astype(BF16) for pr in prs]
        X = [X[pr] + _dot(X[pr].astype(BF16), P16[pr]) for pr in prs]
    U16 = [_dot(X[pr].astype(BF16), (KS[pr][0:2 * C] + LkV[pr]).astype(BF16)).astype(BF16) for pr in prs]
    O = [KS[pr][2 * C:4 * C] + _dot(MkMb[pr], jnp.concatenate([vs16[pr], U16[pr]], axis=0)) for pr in prs]
    upd = [_dot_tn(jnp.concatenate([vs16[pr], -U16[pr]], axis=0), rhs[pr]) for pr in prs]
    for pr in prs:
        o_ref[0, :, sls[pr]] = O[pr][0:C] + O[pr][C:2 * C]
        s_scr[pr] = (S[pr] + upd[pr]) * g_last[:, sls[pr]]

    @pl.when(c_idx == pl.num_programs(1) - 1)
    def _():
        sout_ref[0] = s_scr[...]


def _pair_state(S):
    B = S.shape[0]
    S = S.reshape(B, RWKV_HEADS // 2, 2, RWKV_HD, RWKV_HD)
    z = jnp.zeros_like(S[:, :, 0])
    top = jnp.concatenate([S[:, :, 0], z], axis=-1)
    bot = jnp.concatenate([z, S[:, :, 1]], axis=-1)
    return jnp.concatenate([top, bot], axis=-2)


def _unpair_state(Sbd):
    B = Sbd.shape[0]
    a = Sbd[:, :, :RWKV_HD, :RWKV_HD]
    b = Sbd[:, :, RWKV_HD:, RWKV_HD:]
    return jnp.stack([a, b], axis=2).reshape(B, RWKV_HEADS, RWKV_HD, RWKV_HD)


def _rwkv_fused(p3, shift0, params, S0, chunk):
    B, T, _ = p3.shape
    n_pairs = RWKV_HEADS // 2
    tail_blocks = chunk // V7X_SUBLANES
    const = lambda bb, c: (0, 0)
    tok = pl.BlockSpec((1, chunk, RWKV_W), lambda bb, c: (bb, c, 0))
    st = pl.BlockSpec((1, n_pairs, V7X_LANES, V7X_LANES), lambda bb, c: (bb, 0, 0, 0))
    in_specs = [
        pl.BlockSpec((1, chunk, RWKV_PROJ), lambda bb, c: (bb, c, 0)),
        pl.BlockSpec((1, V7X_SUBLANES, RWKV_PROJ), lambda bb, c: (bb, jnp.maximum(c * tail_blocks - 1, 0), 0)),
        pl.BlockSpec((1, 1, RWKV_PROJ), lambda bb, c: (bb, 0, 0)),
    ] + [pl.BlockSpec(t.shape, const) for t in params] + [st]
    tok_shape = jax.ShapeDtypeStruct((B, T, RWKV_W), F32)
    o, g, bonus, s_out = pl.pallas_call(
        functools.partial(_rwkv_fused_kernel, chunk=chunk),
        grid=(B, T // chunk),
        in_specs=in_specs,
        out_specs=(tok, tok, tok, st),
        out_shape=(tok_shape, tok_shape, tok_shape,
                   jax.ShapeDtypeStruct((B, n_pairs, V7X_LANES, V7X_LANES), F32)),
        scratch_shapes=[pltpu.VMEM((n_pairs, V7X_LANES, V7X_LANES), F32)] + [pltpu.VMEM((chunk, RWKV_W), F32)] * 6,
        compiler_params=_cparams("parallel", "arbitrary"),
        name="rwkv_fused_%d" % chunk,
    )(p3, p3, shift0.reshape(B, 1, RWKV_PROJ), *params, _pair_state(S0))
    return o, g, bonus, _unpair_state(s_out)


def _rwkv_chunked(r, k, v, lw, kk, b, S0, chunk):
    B, T, _ = r.shape
    n_pairs = RWKV_HEADS // 2
    tok = pl.BlockSpec((1, chunk, RWKV_W), lambda bb, c: (bb, c, 0))
    st = pl.BlockSpec((1, n_pairs, V7X_LANES, V7X_LANES), lambda bb, c: (bb, 0, 0, 0))
    o, s_out = pl.pallas_call(
        functools.partial(_rwkv_chunk_kernel, chunk=chunk),
        grid=(B, T // chunk),
        in_specs=[tok] * 6 + [st],
        out_specs=(tok, st),
        out_shape=(jax.ShapeDtypeStruct((B, T, RWKV_W), F32),
                   jax.ShapeDtypeStruct((B, n_pairs, V7X_LANES, V7X_LANES), F32)),
        scratch_shapes=[pltpu.VMEM((n_pairs, V7X_LANES, V7X_LANES), F32)],
        compiler_params=_cparams("parallel", "arbitrary"),
        name="rwkv_chunk_%d" % chunk,
    )(r, k, v, lw, kk, b, _pair_state(S0))
    return o, _unpair_state(s_out)


def _layer_norm(x, g, b):
    mu = jnp.mean(x, axis=-1, keepdims=True)
    var = jnp.mean(jnp.square(x - mu), axis=-1, keepdims=True)
    return (x - mu) * lax.rsqrt(var + LN_EPS) * g + b


def _mix_out_kernel(ga_ref, gr_ref, att_ref, o_ref, bonus_ref, g_ref, x_ref, wout_ref, gng_ref, gnb_ref,
                    ln_g_ref, ln_b_ref, h32_ref, h16_ref):
    ones_bd = _head_ones(V7X_LANES, RWKV_HD)
    o = o_ref[...]
    mean = _segsum(o, ones_bd) * (1.0 / RWKV_HD)
    d = o - mean
    var = _segsum(d * d, ones_bd) * (1.0 / RWKV_HD)
    rw = d * lax.rsqrt(var + RWKV_GN_EPS) * gng_ref[...] + gnb_ref[...]
    rw = (rw + bonus_ref[...]) * g_ref[...]
    mixed_in = jax.nn.sigmoid(ga_ref[...]) * att_ref[...] + jax.nn.sigmoid(gr_ref[...]) * rw
    mixed = _dot(mixed_in.astype(BF16), wout_ref[...])
    h = _layer_norm(DN_ALPHA * x_ref[...] + mixed, ln_g_ref[...], ln_b_ref[...])
    h32_ref[...] = h
    h16_ref[...] = h.astype(BF16)


def _mix_out(gates, att, o, bonus, g, x2d, wout16, gn_g, gn_b, ln_g, ln_b, tm):
    n = x2d.shape[0]
    row = lambda i: (i, 0)
    const = lambda i: (0, 0)
    tok = pl.BlockSpec((tm, D_MODEL), row)
    vec = pl.BlockSpec((1, D_MODEL), const)
    return pl.pallas_call(
        _mix_out_kernel,
        grid=(n // tm,),
        in_specs=[tok, pl.BlockSpec((tm, D_MODEL), lambda i: (i, 1)), tok, tok, tok, tok, tok,
                  pl.BlockSpec((D_MODEL, D_MODEL), const), vec, vec, vec, vec],
        out_specs=(tok, tok),
        out_shape=(jax.ShapeDtypeStruct((n, D_MODEL), F32), jax.ShapeDtypeStruct((n, D_MODEL), BF16)),
        compiler_params=_cparams("parallel"),
        name="mix_out",
    )(gates, gates, att, o, bonus, g, x2d, wout16, gn_g.reshape(1, -1), gn_b.reshape(1, -1),
      ln_g.reshape(1, -1), ln_b.reshape(1, -1))


_STAIR = tuple((p, q) for p in range(PEER_TOPK) for q in range(PEER_TOPK // (p + 1)))
_STAIR_ROWS = -(-len(_STAIR) // V7X_SUBLANES) * V7X_SUBLANES
_RANK_MARK0 = -3.0e38
_RANK_MARK_STEP = 1.0e36
_RANK_PAD = -2.9e38


def _top_ranks(x, k, exact):
    R = x.shape[0]
    row = lax.broadcasted_iota(jnp.int32, x.shape, 0)
    vals = []
    for it in range(k):
        m = jnp.max(x, axis=0, keepdims=True)
        if exact:
            hit = row == jnp.min(jnp.where(x == m, row, R), axis=0, keepdims=True)
        else:
            hit = x == m
        x = jnp.where(hit, _RANK_MARK0 - it * _RANK_MARK_STEP, x)
        vals.append(m)
    marked = x <= _RANK_MARK0
    rank = jnp.where(marked, jnp.floor((_RANK_MARK0 - x) * (1.0 / _RANK_MARK_STEP) + 0.5), float(k))
    count = jnp.sum(jnp.where(marked, 1.0, 0.0), axis=0, keepdims=True)
    return vals, rank, count


def _peer_select_kernel(h_ref, wq_ref, keys_ref, cut_ref, f1_ref, rank2_ref, e2_ref, cand_scr, sel_scr):
    q = _dot(h_ref[...], wq_ref[...]).astype(BF16)
    K = PEER_TOPK

    def select(exact):
        most = jnp.zeros((1, q.shape[0]), F32)
        for h in range(PEER_HEADS):
            s = []
            for c in range(2):
                blk = (h * 2 + c) * PEER_HALF
                s.append(_dot_nt(keys_ref[h * 2 + c], q[:, blk:blk + PEER_HALF]))
            a1, rank1, n1 = _top_ranks(s[0], K, exact)
            a2, rank2, n2 = _top_ranks(s[1], K, exact)
            cand_scr[...] = jnp.full(cand_scr.shape, _RANK_PAD, F32)
            for i, (p, qq) in enumerate(_STAIR):
                cand_scr[i:i + 1, :] = a1[p] + a2[qq]
            cand = cand_scr[...]
            _, crank, nc = _top_ranks(cand, K, exact)
            most = jnp.maximum(most, jnp.maximum(jnp.maximum(n1, n2), nc))
            chosen = crank < float(K)
            z = jnp.sum(jnp.where(chosen, jnp.exp(cand - (a1[0] + a2[0])), 0.0), axis=0, keepdims=True)
            sel_scr[...] = jnp.where(chosen, 1.0, 0.0)
            cut = jnp.zeros(rank1.shape, F32)
            start = 0
            for p in range(K):
                n_p = K // (p + 1)
                cut_p = jnp.sum(sel_scr[start:start + n_p, :], axis=0, keepdims=True)
                cut = jnp.where(rank1 == float(p), cut_p, cut)
                start += n_p
            cut_ref[h] = cut
            f1_ref[h] = jnp.exp(s[0] - a1[0]) * (0.5 / z)
            e2 = jnp.exp(s[1] - a2[0])
            group = rank2_ref.shape[-1]
            for g in range(rank2.shape[1] // group):
                cols = slice(g * group, (g + 1) * group)
                rank2_ref[h, g] = rank2[:, cols].reshape(_KEY_TILES, V7X_SUBLANES, group).astype(BF16)
                e2_ref[h, g] = e2[:, cols].reshape(_KEY_TILES, V7X_SUBLANES, group).astype(BF16)
        return most

    most = select(exact=False)

    @pl.when(jnp.max(most) > float(K))
    def _():
        select(exact=True)


_KEY_TILES = N_KEYS // V7X_SUBLANES


def _peer_group(n):
    return 2 * V7X_LANES if n % (2 * V7X_LANES) == 0 else V7X_LANES


def _peer_select(h16, wq16, keys16, tp):
    n = h16.shape[0]
    group = _peer_group(tp)
    sel_shape = jax.ShapeDtypeStruct((PEER_HEADS, N_KEYS, n), F32)
    sel_spec = pl.BlockSpec((PEER_HEADS, N_KEYS, tp), lambda i: (0, 0, i))
    tab_shape = jax.ShapeDtypeStruct((PEER_HEADS, n // group, _KEY_TILES, V7X_SUBLANES, group), BF16)
    tab_spec = pl.BlockSpec((PEER_HEADS, tp // group, _KEY_TILES, V7X_SUBLANES, group),
                            lambda i: (0, i, 0, 0, 0))
    return pl.pallas_call(
        _peer_select_kernel,
        grid=(n // tp,),
        in_specs=[pl.BlockSpec((tp, D_MODEL), lambda i: (i, 0)),
                  pl.BlockSpec(wq16.shape, lambda i: (0, 0)),
                  pl.BlockSpec(keys16.shape, lambda i: (0, 0, 0))],
        out_specs=(sel_spec, sel_spec, tab_spec, tab_spec),
        out_shape=(sel_shape, sel_shape, tab_shape, tab_shape),
        scratch_shapes=[pltpu.VMEM((_STAIR_ROWS, tp), F32), pltpu.VMEM((_STAIR_ROWS, tp), F32)],
        compiler_params=_cparams("parallel"),
        name="peer_select",
    )(h16, wq16, keys16)


_PEER_I1_PER_BLOCK = 8
_PEER_I1_PER_CHUNK = 2
_SQRT_HALF = math.sqrt(0.5)


def _peer_dense_kernel(x_ref, u_ref, vt_ref, cut_ref, f1_ref, rank2_ref, e2_ref, h_ref, ln_g_ref, ln_b_ref,
                       y_ref, acc_scr, hid_scr, a_scr):
    e = pl.program_id(1)

    @pl.when(e == 0)
    def _():
        acc_scr[...] = jnp.zeros(acc_scr.shape, F32)
        hid_scr[...] = jnp.zeros(hid_scr.shape, F32)

    group = rank2_ref.shape[-1]
    n_groups = x_ref.shape[0] // group
    tile3 = (_KEY_TILES, V7X_SUBLANES, group)

    def stages(cur, prev):
        hid = _dot_nt(u_ref[...], x_ref[...])
        for tc in range(n_groups):
            hid_scr[cur, tc] = hid[:, tc * group:(tc + 1) * group]
        for ch in range(_PEER_I1_PER_BLOCK // _PEER_I1_PER_CHUNK):
            for il in range(ch * _PEER_I1_PER_CHUNK, (ch + 1) * _PEER_I1_PER_CHUNK):
                rows = slice(il * N_KEYS, (il + 1) * N_KEYS)
                for tc in range(n_groups):
                    cols = slice(tc * group, (tc + 1) * group)
                    w = jnp.zeros(tile3, BF16)
                    for h in range(PEER_HEADS):
                        cutb = jnp.broadcast_to(cut_ref[h, il:il + 1, cols], tile3[1:]).astype(BF16)
                        f1b = jnp.broadcast_to(f1_ref[h, il:il + 1, cols], tile3[1:]).astype(BF16)
                        e2 = e2_ref[h, tc]
                        w = w + jnp.where(rank2_ref[h, tc] < cutb[None], e2 * f1b[None], jnp.zeros_like(e2))
                    hb = hid_scr[prev, tc, rows, :]
                    act = (hb * (1.0 + lax.erf(hb * _SQRT_HALF))).reshape(tile3).astype(BF16)
                    a_scr[rows, cols] = (w * act).reshape(N_KEYS, group)
            crow = slice(ch * _PEER_I1_PER_CHUNK * N_KEYS, (ch + 1) * _PEER_I1_PER_CHUNK * N_KEYS)
            acc_scr[...] += _dot(vt_ref[:, crow], a_scr[crow, :])

    for parity in range(2):
        @pl.when(e % 2 == parity)
        def _(parity=parity):
            stages(parity, 1 - parity)

    @pl.when(e == pl.num_programs(1) - 1)
    def _():
        ff = acc_scr[...].T
        y_ref[...] = _layer_norm(DN_ALPHA * h_ref[...] + ff, ln_g_ref[...], ln_b_ref[...])


def _peer_dense(h32, h16, u16, vt16, sel, ln_g, ln_b, tt):
    n = h32.shape[0]
    cut, f1, rank2, e2 = sel
    group = rank2.shape[-1]
    assert tt % group == 0
    eb = _PEER_I1_PER_BLOCK * N_KEYS
    n_e = N_EXPERTS // eb
    tok = lambda i, e: (i, 0)
    const = lambda i, e: (0, 0)
    blk = lambda e, lag: jnp.clip(e - lag, 0, n_e - 1)
    tab_spec = pl.BlockSpec((PEER_HEADS, tt // group, _KEY_TILES, V7X_SUBLANES, group), lambda i, e: (0, i, 0, 0, 0))
    return pl.pallas_call(
        _peer_dense_kernel,
        grid=(n // tt, n_e + 1),
        in_specs=[pl.BlockSpec((tt, D_MODEL), tok),
                  pl.BlockSpec((eb, D_MODEL), lambda i, e: (blk(e, 0), 0)),
                  pl.BlockSpec((None, D_MODEL, eb), lambda i, e: (blk(e, 1), 0, 0)),
                  pl.BlockSpec((PEER_HEADS, _PEER_I1_PER_BLOCK, tt), lambda i, e: (0, blk(e, 1), i)),
                  pl.BlockSpec((PEER_HEADS, _PEER_I1_PER_BLOCK, tt), lambda i, e: (0, blk(e, 1), i)),
                  tab_spec, tab_spec,
                  pl.BlockSpec((tt, D_MODEL), tok),
                  pl.BlockSpec((1, D_MODEL), const), pl.BlockSpec((1, D_MODEL), const)],
        out_specs=pl.BlockSpec((tt, D_MODEL), tok),
        out_shape=jax.ShapeDtypeStruct((n, D_MODEL), F32),
        scratch_shapes=[pltpu.VMEM((D_MODEL, tt), F32), pltpu.VMEM((2, tt // group, eb, group), F32),
                        pltpu.VMEM((eb, tt), BF16)],
        compiler_params=_cparams("parallel", "arbitrary"),
        name="peer_dense",
    )(h16, u16, vt16, cut, f1, rank2, e2, h32, ln_g.reshape(1, -1), ln_b.reshape(1, -1))


_TM_PROMPT = 256
_TQ_PROMPT = 512
_RWKV_CHUNK = 64
_RWKV_CHUNK_SAMPLE = 16
_PAGES_PER_STEP = 8
_PEER_TP = 256
_PEER_TT = 512


def _layer_tail(gates, att2, o2, bonus2, g2, x2, lw, tm, tp, tt):
    h32, h16 = _mix_out(gates, att2, o2, bonus2, g2, x2, lw["wout16"], lw["rwkv_ln_g"], lw["rwkv_ln_b"],
                        lw["ln1_g"], lw["ln1_b"], tm)
    sel = _peer_select(h16, lw["wq16"], lw["keys16"], tp)
    return _peer_dense(h32, h16, lw["u16"], lw["vt16"], sel, lw["ln2_g"], lw["ln2_b"], tt)


def _prompt_layer(x, lw, rel_bias):
    B, T, _ = x.shape
    n = B * T
    x2 = x.reshape(n, D_MODEL)
    q16, k32, k16, v32, v16, gates, p = _in_proj(x2, lw["w_in16"], _TM_PROMPT)
    sh = lambda t: t.reshape(B, T, -1)
    vt16 = jnp.swapaxes(sh(v16), 1, 2)
    att = _prompt_attention(sh(q16), sh(k16), vt16, rel_bias, lw["lam_qk"], lw["attn_norm_g"], _TQ_PROMPT)
    p3 = sh(p)
    shift0 = jnp.zeros((B, RWKV_PROJ), F32)
    S0 = jnp.zeros((B, RWKV_HEADS, RWKV_HD, RWKV_HD), F32)
    o, g, bonus, S_new = _rwkv_fused(p3, shift0, lw["rwkv_params"], S0, _RWKV_CHUNK)
    fl = lambda t: t.reshape(n, -1)
    y = _layer_tail(gates, fl(att), fl(o), fl(bonus), fl(g), x2, lw, _TM_PROMPT, _PEER_TP, _PEER_TT)
    return (y.reshape(B, T, D_MODEL), k32.reshape(B, T, ATT_HEADS, 2 * ATT_HD), v32.reshape(B, T, ATT_HEADS, ATT_VD),
            S_new, p3[:, -1, :])


def _sample_layer(x, lw, rel_bias, cache_k, cache_v, layer, page_table, shift0, S0):
    DB, S, _ = x.shape
    n = DB * S
    x2 = x.reshape(n, D_MODEL)
    q16, k32, k16, v32, v16, gates, p = _in_proj(x2, lw["w_in16"], n)
    sh = lambda t: t.reshape(DB, S, -1)
    att = _sample_attention(sh(q16), sh(k32), sh(v32), cache_k, cache_v, layer, page_table, rel_bias,
                            lw["lam_qk"], lw["attn_norm_g"], _PAGES_PER_STEP)
    p3 = sh(p)
    prev = jnp.concatenate([shift0[:, None, :], p3[:, :-1, :]], axis=1).reshape(n, RWKV_PROJ)
    r, k, v, lgw, kk, b, g, bonus = _rwkv_prep_sample(p, prev, lw["rwkv_params"])
    pad = lambda t: jnp.pad(sh(t), ((0, 0), (0, _RWKV_CHUNK_SAMPLE - S), (0, 0)))
    o, S_new = _rwkv_chunked(pad(r), pad(k), pad(v), pad(lgw), pad(kk), pad(b), S0, _RWKV_CHUNK_SAMPLE)
    o2 = o[:, :S, :].reshape(n, RWKV_W)
    y = _layer_tail(gates, att.reshape(n, ATT_V), o2, bonus, g, x2, lw, n, n, n)
    return (y.reshape(DB, S, D_MODEL), k32.reshape(DB, S, ATT_HEADS, 2 * ATT_HD), v32.reshape(DB, S, ATT_HEADS, ATT_VD),
            S_new, p3[:, -1, :])


def kernel(x_prompt, x_sample, cache_k, cache_v, state_wkv, state_shift, page_table, w_in, w_out, lam_qk, attn_norm_g, shift_mu, rwkv_w0, rwkv_w2, rwkv_a0, rwkv_a2, rwkv_g2, rwkv_kk, rwkv_ka, rwkv_rk, rwkv_ln_g, rwkv_ln_b, ln1_g, ln1_b, ln2_g, ln2_b, peer_wq, peer_keys, peer_u, peer_v, rel_bias):
    assert w_in.shape[0] == DEPTH == 1
    yp, ys = x_prompt, x_sample
    outs = [[] for _ in range(8)]
    for l in range(DEPTH):
        lw = {
            "w_in16": w_in[l].astype(BF16), "wout16": w_out[l].astype(BF16),
            "lam_qk": lam_qk[l], "attn_norm_g": attn_norm_g[l],
            "rwkv_params": _rwkv_param_list(shift_mu[l], rwkv_w0[l], rwkv_w2[l], rwkv_a0[l], rwkv_a2[l],
                                            rwkv_g2[l], rwkv_kk[l], rwkv_ka[l], rwkv_rk[l]),
            "rwkv_ln_g": rwkv_ln_g[l], "rwkv_ln_b": rwkv_ln_b[l],
            "ln1_g": ln1_g[l], "ln1_b": ln1_b[l], "ln2_g": ln2_g[l], "ln2_b": ln2_b[l],
            "wq16": peer_wq[l].astype(BF16),
            "keys16": peer_keys[l].reshape(PEER_HEADS * 2, N_KEYS, PEER_HALF).astype(BF16),
            "u16": peer_u[l].astype(BF16),
            "vt16": jnp.swapaxes(peer_v[l].astype(BF16).reshape(-1, _PEER_I1_PER_BLOCK * N_KEYS, D_MODEL), 1, 2),
        }
        yp, kp, vp, Sp, shp = _prompt_layer(yp, lw, rel_bias)
        ys, ksm, vsm, Ss, shs = _sample_layer(ys, lw, rel_bias, cache_k, cache_v, l, page_table,
                                              state_shift[l], state_wkv[l])
        for lst, val in zip(outs, (kp, vp, ksm, vsm, Sp, Ss, shp, shs)):
            lst.append(val)
    stacked = tuple(jnp.stack(lst, axis=0) for lst in outs)
    return (yp, ys) + stacked
```

```python
import functools
import math

import jax
import jax.numpy as jnp
from jax import lax
from jax.experimental import pallas as pl
from jax.experimental.pallas import tpu as pltpu

D_MODEL = 1024
PAGE_SIZE = 128
ATT_HD = 64
ATT_HEADS = D_MODEL // (2 * ATT_HD)
ATT_VD = 2 * ATT_HD
ATT_QK = ATT_HEADS * 2 * ATT_HD
ATT_V = ATT_HEADS * ATT_VD
N_BUCKETS = 32
MAX_DISTANCE = 128
ATT_EPS = 1e-5
RWKV_HD = 64
RWKV_HEADS = D_MODEL // RWKV_HD
RWKV_W = RWKV_HEADS * RWKV_HD
DECAY_LORA = 64
AAA_LORA = 64
GATE_LORA = 128
RWKV_PROJ = 3 * RWKV_W + DECAY_LORA + AAA_LORA + GATE_LORA
DECAY_SCALE = 0.606531
RWKV_GN_EPS = 64e-5
O_K = ATT_QK
O_V = O_K + ATT_QK
O_GA = O_V + ATT_V
O_GR = O_GA + D_MODEL
O_RW = O_GR + D_MODEL
N_IN = O_RW + RWKV_PROJ
PEER_HEADS = 8
N_KEYS = 128
N_EXPERTS = N_KEYS * N_KEYS
PEER_QDIM = 256
PEER_HALF = PEER_QDIM // 2
PEER_TOPK = 16
DEPTH = 1
DN_ALPHA = (2.0 * DEPTH) ** 0.25
LN_EPS = 1e-5
LAMBDA_INIT = 0.8 - 0.6 * math.exp(-0.3 * 0)

V7X_LANES = 128
V7X_SUBLANES = 8
V7X_VMEM_LIMIT_BYTES = 56 * 1024 * 1024

NEG_INF = -1e30
LOG2E = math.log2(math.e)
BF16 = jnp.bfloat16
F32 = jnp.float32

NT_DIMS = (((1,), (1,)), ((), ()))
TN_DIMS = (((0,), (0,)), ((), ()))


def _cparams(*sem, flags=None):
    return pltpu.CompilerParams(dimension_semantics=sem, vmem_limit_bytes=V7X_VMEM_LIMIT_BYTES, flags=flags)


def _dot(a, b):
    return jnp.dot(a, b, preferred_element_type=F32)


def _dot_nt(a, b):
    return lax.dot_general(a, b, NT_DIMS, preferred_element_type=F32)


def _dot_tn(a, b):
    return lax.dot_general(a, b, TN_DIMS, preferred_element_type=F32)


def _split_dot(x, w_bf16):
    hi = x.astype(BF16)
    lo = (x - hi.astype(F32)).astype(BF16)
    return _dot(hi, w_bf16) + _dot(lo, w_bf16)


_P_CHUNKS = ((0, 1024), (1024, 2048), (2048, 3072), (3072, RWKV_PROJ))


def _in_proj_kernel(x_ref, w_ref, q16_ref, k32_ref, k16_ref, v32_ref, v16_ref, gates_ref, p_ref):
    x = x_ref[...].astype(BF16)

    def mm(lo, hi):
        return _dot(x, w_ref[:, lo:hi])

    q16_ref[...] = (mm(0, O_K) * (ATT_HD ** -0.5 * LOG2E)).astype(BF16)
    k = mm(O_K, O_V)
    k32_ref[...] = k
    k16_ref[...] = k.astype(BF16)
    v = mm(O_V, O_GA)
    v32_ref[...] = v
    v16_ref[...] = v.astype(BF16)
    gates_ref[:, 0:D_MODEL] = mm(O_GA, O_GR)
    gates_ref[:, D_MODEL:2 * D_MODEL] = mm(O_GR, O_RW)
    for lo, hi in _P_CHUNKS:
        p_ref[:, lo:hi] = mm(O_RW + lo, O_RW + hi)


def _in_proj(x2d, w_bf16, tm):
    n = x2d.shape[0]
    row = lambda i: (i, 0)
    full = lambda i: (0, 0)
    out_shape = (
        jax.ShapeDtypeStruct((n, ATT_QK), BF16),
        jax.ShapeDtypeStruct((n, ATT_QK), F32),
        jax.ShapeDtypeStruct((n, ATT_QK), BF16),
        jax.ShapeDtypeStruct((n, ATT_V), F32),
        jax.ShapeDtypeStruct((n, ATT_V), BF16),
        jax.ShapeDtypeStruct((n, 2 * D_MODEL), F32),
        jax.ShapeDtypeStruct((n, RWKV_PROJ), F32),
    )
    out_specs = tuple(pl.BlockSpec((tm, s.shape[1]), row) for s in out_shape)
    return pl.pallas_call(
        _in_proj_kernel,
        grid=(n // tm,),
        in_specs=[pl.BlockSpec((tm, D_MODEL), row),
                  pl.BlockSpec((D_MODEL, N_IN), full, pipeline_mode=pl.Buffered(1))],
        out_specs=out_specs,
        out_shape=out_shape,
        compiler_params=_cparams("parallel"),
        name="in_proj",
    )(x2d, w_bf16)


def _t5_bucket(rel):
    n = jnp.maximum(rel, 0)
    max_exact = N_BUCKETS // 2
    nf = jnp.maximum(n, 1).astype(F32)
    large = max_exact + (jnp.log(nf / max_exact) / math.log(MAX_DISTANCE / max_exact)
                         * (N_BUCKETS - max_exact)).astype(jnp.int32)
    large = jnp.minimum(large, N_BUCKETS - 1)
    return jnp.where(n < max_exact, n, large)


def _shifted_bias(rel_bias, dist):
    far = lax.dynamic_index_in_dim(rel_bias, _t5_bucket(jnp.int32(2 * MAX_DISTANCE)), 0, keepdims=False)
    table = (rel_bias - far[None, :]) * LOG2E
    bucket = _t5_bucket(dist)[None]
    expand = (slice(None),) + (None,) * dist.ndim
    out = jnp.zeros((rel_bias.shape[1],) + dist.shape, F32)
    for b in range(N_BUCKETS):
        out = jnp.where(bucket == b, table[b][expand], out)
    return out


def _per_row_head(tiles, row_h):
    out = jnp.zeros(tiles.shape[1:], F32)
    for h in range(tiles.shape[0]):
        out = jnp.where(row_h[:, None] == h, tiles[h], out)
    return out


def _lambda_value(lq):
    s01 = jnp.sum(lq[0:1] * lq[1:2], axis=-1, keepdims=True)
    s23 = jnp.sum(lq[2:3] * lq[3:4], axis=-1, keepdims=True)
    return jnp.exp(s01) - jnp.exp(s23) + LAMBDA_INIT


def _rms_scale(o):
    return o * lax.rsqrt(jnp.mean(o * o, axis=-1, keepdims=True) + ATT_EPS)


def _prompt_attn_kernel(q_ref, k_ref, vt_ref, bias_ref, lam_ref, g_ref, o_ref, m_ref, l_ref, acc_ref,
                        sa_ref, sb_ref, *, tq):
    i = pl.program_id(2)
    tk = tq // 2
    lane = lax.broadcasted_iota(jnp.int32, (tq, 2 * ATT_HD), 1)
    q = q_ref[0]
    zero = jnp.zeros_like(q)
    qm = (jnp.where(lane < ATT_HD, q, zero), jnp.where(lane >= ATT_HD, q, zero))

    m_ref[...] = jnp.full(m_ref.shape, NEG_INF, F32)
    l_ref[...] = jnp.zeros(l_ref.shape, F32)
    acc_ref[...] = jnp.zeros(acc_ref.shape, F32)

    def scores(b, s_ref):
        kb = k_ref[0, pl.ds(pl.multiple_of(b * tk, tk), tk), :]
        for c in range(2):
            s_ref[c] = _dot_nt(kb, qm[c])

    def consume(b, s_ref, add_t):
        vt = vt_ref[0, :, pl.ds(pl.multiple_of(b * tk, tk), tk)]
        p16, alphas = [], []
        for c in range(2):
            sc = s_ref[c] if add_t is None else s_ref[c] + add_t
            m_old = m_ref[c]
            m_new = jnp.maximum(m_old, jnp.max(sc, axis=0, keepdims=True))
            alpha = jnp.exp2(m_old - m_new)
            p = jnp.exp2(sc - m_new)
            l_ref[c] = alpha * l_ref[c] + jnp.sum(p, axis=0, keepdims=True)
            m_ref[c] = m_new
            p16.append(p.astype(BF16))
            alphas.append(alpha)
        for c in range(2):
            acc_ref[c] = alphas[c] * acc_ref[c] + _dot(vt, p16[c])

    scores(0, sa_ref)

    def far_body(p, carry):
        scores(2 * p + 1, sb_ref)
        consume(2 * p, sa_ref, None)
        scores(2 * p + 2, sa_ref)
        consume(2 * p + 1, sb_ref, None)
        return carry

    lax.fori_loop(0, jnp.maximum(i - 1, 0), far_body, 0)

    @pl.when(i > 0)
    def _():
        scores(2 * i - 1, sb_ref)
        consume(2 * i - 2, sa_ref, None)
        scores(2 * i, sa_ref)
        consume(2 * i - 1, sb_ref, bias_ref[0, 0])

    scores(2 * i + 1, sb_ref)
    consume(2 * i, sa_ref, bias_ref[0, 1])
    consume(2 * i + 1, sb_ref, bias_ref[0, 2])

    lam = _lambda_value(lam_ref[...])
    ot = acc_ref[0] / l_ref[0] - lam * (acc_ref[1] / l_ref[1])
    ot = ot * lax.rsqrt(jnp.mean(ot * ot, axis=0, keepdims=True) + ATT_EPS)
    o_ref[0] = ot.T * (g_ref[...] * (1.0 - LAMBDA_INIT))


def _prompt_attention(q16, k16, vt16, rel_bias, lam_qk, attn_norm_g, tq):
    B, T, _ = q16.shape
    tk = tq // 2
    assert tk >= MAX_DISTANCE
    diff_t = jnp.arange(tq, dtype=jnp.int32)[None, :] - jnp.arange(tk, dtype=jnp.int32)[:, None]
    tiles = []
    for off in (tk, 0, -tk):
        dist = diff_t + off
        tiles.append(jnp.where(dist >= 0, _shifted_bias(rel_bias, dist), NEG_INF))
    bias = jnp.stack(tiles, axis=1).astype(F32)
    kern = functools.partial(_prompt_attn_kernel, tq=tq)
    return pl.pallas_call(
        kern,
        grid=(B, ATT_HEADS, T // tq),
        in_specs=[
            pl.BlockSpec((1, tq, 2 * ATT_HD), lambda b, h, i: (b, i, h)),
            pl.BlockSpec((1, T, 2 * ATT_HD), lambda b, h, i: (b, 0, h)),
            pl.BlockSpec((1, ATT_VD, T), lambda b, h, i: (b, h, 0)),
            pl.BlockSpec((1, 3, tk, tq), lambda b, h, i: (h, 0, 0, 0)),
            pl.BlockSpec((4, ATT_HD), lambda b, h, i: (0, 0)),
            pl.BlockSpec((1, ATT_VD), lambda b, h, i: (0, h)),
        ],
        out_specs=pl.BlockSpec((1, tq, ATT_VD), lambda b, h, i: (b, i, h)),
        out_shape=jax.ShapeDtypeStruct((B, T, ATT_V), F32),
        scratch_shapes=[pltpu.VMEM((2, 1, tq), F32), pltpu.VMEM((2, 1, tq), F32),
                        pltpu.VMEM((2, ATT_VD, tq), F32),
                        pltpu.VMEM((2, tk, tq), F32), pltpu.VMEM((2, tk, tq), F32)],
        compiler_params=_cparams("parallel", "parallel", "arbitrary"),
        name="prompt_attention",
    )(q16, k16, vt16, bias, lam_qk, attn_norm_g.reshape(1, ATT_V))


def _sample_attn_kernel(pt_ref, q_ref, plain_ref, last_ref, new_ref, knew_ref, vnew_ref, lam_ref, g_ref,
                        *rest, pages_per_step):
    k_refs = rest[:pages_per_step]
    v_refs = rest[pages_per_step:2 * pages_per_step]
    o_ref, m_ref, l_ref, acc_ref = rest[2 * pages_per_step:]
    j = pl.program_id(1)
    last = pl.num_programs(1) - 1
    q = q_ref[0]

    @pl.when(j == 0)
    def _():
        m_ref[...] = jnp.full(m_ref.shape, NEG_INF, F32)
        l_ref[...] = jnp.zeros(l_ref.shape, F32)
        acc_ref[...] = jnp.zeros(acc_ref.shape, F32)

    def update(s, v2d):
        m_old = m_ref[...]
        m_new = jnp.maximum(m_old, jnp.max(s, axis=-1, keepdims=True))
        alpha = jnp.exp2(m_old - m_new)
        p = jnp.exp2(s - m_new)
        l_ref[...] = alpha * l_ref[...] + jnp.sum(p, axis=-1, keepdims=True)
        acc_ref[...] = alpha * acc_ref[...] + _dot(p.astype(BF16), v2d)
        m_ref[...] = m_new

    rows2d = PAGE_SIZE * ATT_HEADS
    k_all = jnp.concatenate([r[...].reshape(rows2d, 2 * ATT_HD).astype(BF16) for r in k_refs], axis=0)
    v_all = jnp.concatenate([r[...].reshape(rows2d, ATT_VD).astype(BF16) for r in v_refs], axis=0)
    add = jnp.concatenate([plain_ref[...]] * (pages_per_step - 1) + [last_ref[0]], axis=1)
    update(_dot_nt(q, k_all) + add, v_all)

    @pl.when(j == last)
    def _():
        n_new = knew_ref.shape[1]
        k2d = knew_ref[0].reshape(n_new * ATT_HEADS, 2 * ATT_HD).astype(BF16)
        v2d = vnew_ref[0].reshape(n_new * ATT_HEADS, ATT_VD).astype(BF16)
        update(_dot_nt(q, k2d) + new_ref[...], v2d)
        half = acc_ref.shape[0] // 2
        o = acc_ref[...] / l_ref[...]
        lam = _lambda_value(lam_ref[...])
        diff = o[0:half] - lam * o[half:2 * half]
        o_ref[0] = _rms_scale(diff) * (g_ref[...] * (1.0 - LAMBDA_INIT))


def _sample_attention(q16, k32, v32, cache_k, cache_v, layer, page_table, rel_bias, lam_qk, attn_norm_g,
                      pages_per_step):
    DB, S, _ = q16.shape
    n_pages = page_table.shape[1]
    past = n_pages * PAGE_SIZE
    H = ATT_HEADS
    rows = 2 * S * H
    q5 = q16.reshape(DB, S, H, 2, ATT_HD)
    qrows = jnp.stack([jnp.concatenate([q5[:, :, :, 0], jnp.zeros_like(q5[:, :, :, 0])], axis=-1),
                       jnp.concatenate([jnp.zeros_like(q5[:, :, :, 1]), q5[:, :, :, 1]], axis=-1)], axis=1)
    qrows = qrows.reshape(DB, rows, 2 * ATT_HD)
    row_s = (jnp.arange(rows, dtype=jnp.int32) // H) % S
    row_h = jnp.arange(rows, dtype=jnp.int32) % H
    col_t = jnp.arange(PAGE_SIZE * H, dtype=jnp.int32) // H
    col_h = jnp.arange(PAGE_SIZE * H, dtype=jnp.int32) % H
    same_head = row_h[:, None] == col_h[None, :]
    plain = jnp.where(same_head, 0.0, NEG_INF).astype(F32)
    dist_last = PAGE_SIZE + row_s[:, None] - col_t[None, :]
    b_last = _per_row_head(_shifted_bias(rel_bias, dist_last), row_h)
    last_tile = jnp.where(same_head, b_last, NEG_INF).astype(F32)
    tiles = jnp.stack([plain, last_tile], axis=0)
    ncol_t = jnp.arange(S * H, dtype=jnp.int32) // H
    ncol_h = jnp.arange(S * H, dtype=jnp.int32) % H
    dist_new = row_s[:, None] - ncol_t[None, :]
    b_new = _per_row_head(_shifted_bias(rel_bias, dist_new), row_h)
    ok_new = (row_h[:, None] == ncol_h[None, :]) & (dist_new >= 0)
    new_tile = jnp.where(ok_new, b_new, NEG_INF).astype(F32)
    g_tile = jnp.tile(attn_norm_g.reshape(H, ATT_VD), (S, 1))

    assert n_pages % pages_per_step == 0
    n_steps = n_pages // pages_per_step
    kern = functools.partial(_sample_attn_kernel, pages_per_step=pages_per_step)

    def page_spec(g):
        return pl.BlockSpec((None, None, PAGE_SIZE, H, 2 * ATT_HD),
                            lambda b, j, pt, g=g: (layer, pt[b, j * pages_per_step + g], 0, 0, 0))

    grid_spec = pltpu.PrefetchScalarGridSpec(
        num_scalar_prefetch=1,
        grid=(DB, n_steps),
        in_specs=[
            pl.BlockSpec((1, rows, 2 * ATT_HD), lambda b, j, pt: (b, 0, 0)),
            pl.BlockSpec((None, rows, PAGE_SIZE * H), lambda b, j, pt: (0, 0, 0)),
            pl.BlockSpec((1, rows, PAGE_SIZE * H), lambda b, j, pt: (jnp.where(j == n_steps - 1, 1, 0), 0, 0)),
            pl.BlockSpec((rows, S * H), lambda b, j, pt: (0, 0)),
            pl.BlockSpec((1, S, H, 2 * ATT_HD), lambda b, j, pt: (b, 0, 0, 0)),
            pl.BlockSpec((1, S, H, ATT_VD), lambda b, j, pt: (b, 0, 0, 0)),
            pl.BlockSpec((4, ATT_HD), lambda b, j, pt: (0, 0)),
            pl.BlockSpec((S * H, ATT_VD), lambda b, j, pt: (0, 0)),
        ] + [page_spec(g) for g in range(pages_per_step)] + [page_spec(g) for g in range(pages_per_step)],
        out_specs=pl.BlockSpec((1, S * H, ATT_VD), lambda b, j, pt: (b, 0, 0)),
        scratch_shapes=[pltpu.VMEM((rows, 1), F32), pltpu.VMEM((rows, 1), F32), pltpu.VMEM((rows, ATT_VD), F32)],
    )
    out = pl.pallas_call(
        kern,
        grid_spec=grid_spec,
        out_shape=jax.ShapeDtypeStruct((DB, S * H, ATT_VD), F32),
        compiler_params=_cparams("parallel", "arbitrary"),
        name="sample_attention",
    )(page_table, qrows, tiles, tiles, new_tile,
      k32.reshape(DB, S, H, 2 * ATT_HD), v32.reshape(DB, S, H, ATT_VD), lam_qk, g_tile,
      *([cache_k] * pages_per_step), *([cache_v] * pages_per_step))
    return out.reshape(DB, S, ATT_V)


def _head_ones(n, seg):
    r = lax.broadcasted_iota(jnp.int32, (n, n), 0) // seg
    c = lax.broadcasted_iota(jnp.int32, (n, n), 1) // seg
    return jnp.where(r == c, 1.0, 0.0).astype(BF16)


def _segsum(x, ones_bd):
    outs = []
    for g in range(x.shape[1] // V7X_LANES):
        outs.append(_split_dot(x[:, g * V7X_LANES:(g + 1) * V7X_LANES], ones_bd))
    return jnp.concatenate(outs, axis=1)


def _rwkv_prep_core(p, prev, mu_ref, w0_ref, w2_ref, a0_ref, a2_ref, g2_ref, kks_ref, ka_ref, rk_ref, outs):
    r_ref, k_ref, v_ref, lw_ref, kk_ref, b_ref, g_ref, bonus_ref = outs
    W = RWKV_W
    xs = p + (prev - p) * mu_ref[...]
    r = xs[:, 0:W]
    k = xs[:, W:2 * W]
    v = xs[:, 2 * W:3 * W]
    wa = xs[:, 3 * W:3 * W + DECAY_LORA + AAA_LORA]
    gd = xs[:, 3 * W + DECAY_LORA + AAA_LORA:RWKV_PROJ]
    logw = -DECAY_SCALE * jax.nn.sigmoid(w0_ref[...] + _dot(jnp.tanh(wa).astype(BF16), w2_ref[...]))
    a = jax.nn.sigmoid(a0_ref[...] + _dot(wa.astype(BF16), a2_ref[...]))
    g = _dot(jax.nn.sigmoid(gd).astype(BF16), g2_ref[...])
    ones_bd = _head_ones(V7X_LANES, RWKV_HD)
    kk = k * kks_ref[...]
    kk = kk / jnp.maximum(jnp.sqrt(_segsum(kk * kk, ones_bd)), 1e-12)
    k = k * (1.0 + (a - 1.0) * ka_ref[...])
    r_ref[...] = r
    k_ref[...] = k
    v_ref[...] = v
    lw_ref[...] = logw
    kk_ref[...] = kk
    b_ref[...] = kk * a
    g_ref[...] = g
    bonus_ref[...] = _segsum(r * k * rk_ref[...], ones_bd) * v


def _rwkv_prep_sample_kernel(p_ref, prev_ref, *rest):
    params, outs = rest[:9], rest[9:]
    _rwkv_prep_core(p_ref[...], prev_ref[...], *params, outs)


def _rwkv_param_list(shift_mu, rwkv_w0, rwkv_w2, rwkv_a0, rwkv_a2, rwkv_g2, rwkv_kk, rwkv_ka, rwkv_rk):
    zeros = jnp.zeros((AAA_LORA, RWKV_W), F32)
    w2p = jnp.concatenate([rwkv_w2, zeros], axis=0).astype(BF16)
    a2p = jnp.concatenate([jnp.zeros((DECAY_LORA, RWKV_W), F32), rwkv_a2], axis=0).astype(BF16)
    row = lambda t: t.reshape(1, -1)
    return [row(shift_mu), row(rwkv_w0), w2p, row(rwkv_a0), a2p, rwkv_g2.astype(BF16),
            row(rwkv_kk), row(rwkv_ka), row(rwkv_rk)]


def _rwkv_prep_sample(p2, prev2, params):
    n = p2.shape[0]
    const = lambda i: (0, 0)
    in_specs = [pl.BlockSpec((n, RWKV_PROJ), const), pl.BlockSpec((n, RWKV_PROJ), const)] + [
        pl.BlockSpec(t.shape, const) for t in params]
    out_shape = tuple(jax.ShapeDtypeStruct((n, RWKV_W), F32) for _ in range(8))
    out_specs = tuple(pl.BlockSpec((n, RWKV_W), const) for _ in range(8))
    return pl.pallas_call(
        _rwkv_prep_sample_kernel,
        grid=(1,),
        in_specs=in_specs, out_specs=out_specs, out_shape=out_shape,
        compiler_params=_cparams("arbitrary"),
        name="rwkv_prep_sample",
    )(p2, prev2, *params)


def _rwkv_chunk_kernel(r_ref, k_ref, v_ref, lw_ref, kk_ref, b_ref, s0_ref, o_ref, sout_ref, s_scr, *, chunk):
    _rwkv_chunk_core(r_ref[0], k_ref[0], v_ref[0], lw_ref[0], kk_ref[0], b_ref[0], s0_ref, o_ref, sout_ref, s_scr,
                     chunk)


def _rwkv_fused_kernel(p_ref, tail_ref, shift_ref, *rest, chunk):
    params, rest = rest[:9], rest[9:]
    s0_ref, o_ref, g_ref, bonus_ref, sout_ref, s_scr = rest[:6]
    r_s, k_s, v_s, lw_s, kk_s, b_s = rest[6:]
    c_idx = pl.program_id(1)
    p = p_ref[0]
    first = jnp.where(c_idx == 0, shift_ref[0], tail_ref[0, V7X_SUBLANES - 1:V7X_SUBLANES, :])
    row = lax.broadcasted_iota(jnp.int32, p.shape, 0)
    prev = jnp.where(row == 0, first, pltpu.roll(p, 1, 0))
    _rwkv_prep_core(p, prev, *params, [r_s, k_s, v_s, lw_s, kk_s, b_s, g_ref.at[0], bonus_ref.at[0]])
    _rwkv_chunk_core(r_s[...], k_s[...], v_s[...], lw_s[...], kk_s[...], b_s[...], s0_ref, o_ref, sout_ref, s_scr,
                     chunk)


def _rwkv_chunk_core(r_in, k_in, v_in, lw, kk_in, b_in, s0_ref, o_ref, sout_ref, s_scr, chunk):
    C = chunk
    c_idx = pl.program_id(1)
    n_pairs = RWKV_HEADS // 2
    L = V7X_LANES

    @pl.when(c_idx == 0)
    def _():
        s_scr[...] = s0_ref[0]

    ti = lax.broadcasted_iota(jnp.int32, (C, C), 0)
    si = lax.broadcasted_iota(jnp.int32, (C, C), 1)
    tri = jnp.where(si <= ti, 1.0, 0.0).astype(BF16)
    hi = lw.astype(BF16)
    rem = lw - hi.astype(F32)
    mid = rem.astype(BF16)
    lo = (rem - mid.astype(F32)).astype(BF16)
    cs = _dot(tri, hi) + _dot(tri, mid) + _dot(tri, lo)
    g = jnp.exp(cs)
    gprev = jnp.exp(cs - lw)
    ginv = jnp.exp(-cs)
    kap_all = kk_in * gprev
    rt_all = r_in * g
    kt_all = k_in * ginv
    bt_all = b_in * ginv
    v_all = v_in
    g_last = g[C - 1:C, :]

    lane = lax.broadcasted_iota(jnp.int32, (C, L), 1)
    in_h0 = lane < RWKV_HD

    def stack(x):
        z = jnp.zeros_like(x)
        return jnp.concatenate([jnp.where(in_h0, x, z), jnp.where(in_h0, z, x)], axis=0)

    rr = lax.broadcasted_iota(jnp.int32, (2 * C, 2 * C), 0)
    cc = lax.broadcasted_iota(jnp.int32, (2 * C, 2 * C), 1)
    same = (rr // C) == (cc // C)
    strict = same & (cc < rr)
    incl = same & (cc <= rr)
    eye = jnp.where(rr == cc, 1.0, 0.0).astype(F32)
    n_sq = max(int(math.ceil(math.log2(C))) - 1, 0)

    prs = range(n_pairs)
    sls = [slice(pr * L, (pr + 1) * L) for pr in prs]
    vs16 = [stack(v_all[:, sl]).astype(BF16) for sl in sls]
    lhs = [jnp.concatenate([stack(kap_all[:, sl]).astype(BF16), stack(rt_all[:, sl]).astype(BF16)], axis=0)
           for sl in sls]
    rhs = [jnp.concatenate([stack(kt_all[:, sl]).astype(BF16), stack(bt_all[:, sl]).astype(BF16)], axis=0)
           for sl in sls]
    A = [_dot_nt(lhs[pr], rhs[pr]) for pr in prs]
    S = [s_scr[pr] for pr in prs]
    KS = [_dot_nt(lhs[pr], S[pr].astype(BF16)) for pr in prs]
    zero = jnp.zeros((2 * C, 2 * C), F32)
    Lk16 = [jnp.where(strict, A[pr][0:2 * C, 0:2 * C], zero).astype(BF16) for pr in prs]
    Lb = [jnp.where(strict, A[pr][0:2 * C, 2 * C:4 * C], zero) for pr in prs]
    MkMb = [jnp.concatenate([jnp.where(incl, A[pr][2 * C:4 * C, 0:2 * C], zero),
                             -jnp.where(incl, A[pr][2 * C:4 * C, 2 * C:4 * C], zero)], axis=1).astype(BF16)
            for pr in prs]
    LkV = [_dot(Lk16[pr], vs16[pr]) for pr in prs]
    X = [eye - Lb[pr] for pr in prs]
    P16 = [Lb[pr].astype(BF16) for pr in prs]
    for _ in range(n_sq):
        P16 = [_dot(P16[pr], P16[pr]).astype(BF16) for pr in prs]
        X = [X[pr] + _dot(X[pr].astype(BF16), P16[pr]) for pr in prs]
    U16 = [_dot(X[pr].astype(BF16), (KS[pr][0:2 * C] + LkV[pr]).astype(BF16)).astype(BF16) for pr in prs]
    O = [KS[pr][2 * C:4 * C] + _dot(MkMb[pr], jnp.concatenate([vs16[pr], U16[pr]], axis=0)) for pr in prs]
    upd = [_dot_tn(jnp.concatenate([vs16[pr], -U16[pr]], axis=0), rhs[pr]) for pr in prs]
    for pr in prs:
        o_ref[0, :, sls[pr]] = O[pr][0:C] + O[pr][C:2 * C]
        s_scr[pr] = (S[pr] + upd[pr]) * g_last[:, sls[pr]]

    @pl.when(c_idx == pl.num_programs(1) - 1)
    def _():
        sout_ref[0] = s_scr[...]


def _pair_state(S):
    B = S.shape[0]
    S = S.reshape(B, RWKV_HEADS // 2, 2, RWKV_HD, RWKV_HD)
    z = jnp.zeros_like(S[:, :, 0])
    top = jnp.concatenate([S[:, :, 0], z], axis=-1)
    bot = jnp.concatenate([z, S[:, :, 1]], axis=-1)
    return jnp.concatenate([top, bot], axis=-2)


def _unpair_state(Sbd):
    B = Sbd.shape[0]
    a = Sbd[:, :, :RWKV_HD, :RWKV_HD]
    b = Sbd[:, :, RWKV_HD:, RWKV_HD:]
    return jnp.stack([a, b], axis=2).reshape(B, RWKV_HEADS, RWKV_HD, RWKV_HD)


def _rwkv_fused(p3, shift0, params, S0, chunk):
    B, T, _ = p3.shape
    n_pairs = RWKV_HEADS // 2
    tail_blocks = chunk // V7X_SUBLANES
    const = lambda bb, c: (0, 0)
    tok = pl.BlockSpec((1, chunk, RWKV_W), lambda bb, c: (bb, c, 0))
    st = pl.BlockSpec((1, n_pairs, V7X_LANES, V7X_LANES), lambda bb, c: (bb, 0, 0, 0))
    in_specs = [
        pl.BlockSpec((1, chunk, RWKV_PROJ), lambda bb, c: (bb, c, 0)),
        pl.BlockSpec((1, V7X_SUBLANES, RWKV_PROJ), lambda bb, c: (bb, jnp.maximum(c * tail_blocks - 1, 0), 0)),
        pl.BlockSpec((1, 1, RWKV_PROJ), lambda bb, c: (bb, 0, 0)),
    ] + [pl.BlockSpec(t.shape, const) for t in params] + [st]
    tok_shape = jax.ShapeDtypeStruct((B, T, RWKV_W), F32)
    o, g, bonus, s_out = pl.pallas_call(
        functools.partial(_rwkv_fused_kernel, chunk=chunk),
        grid=(B, T // chunk),
        in_specs=in_specs,
        out_specs=(tok, tok, tok, st),
        out_shape=(tok_shape, tok_shape, tok_shape,
                   jax.ShapeDtypeStruct((B, n_pairs, V7X_LANES, V7X_LANES), F32)),
        scratch_shapes=[pltpu.VMEM((n_pairs, V7X_LANES, V7X_LANES), F32)] + [pltpu.VMEM((chunk, RWKV_W), F32)] * 6,
        compiler_params=_cparams("parallel", "arbitrary"),
        name="rwkv_fused_%d" % chunk,
    )(p3, p3, shift0.reshape(B, 1, RWKV_PROJ), *params, _pair_state(S0))
    return o, g, bonus, _unpair_state(s_out)


def _rwkv_chunked(r, k, v, lw, kk, b, S0, chunk):
    B, T, _ = r.shape
    n_pairs = RWKV_HEADS // 2
    tok = pl.BlockSpec((1, chunk, RWKV_W), lambda bb, c: (bb, c, 0))
    st = pl.BlockSpec((1, n_pairs, V7X_LANES, V7X_LANES), lambda bb, c: (bb, 0, 0, 0))
    o, s_out = pl.pallas_call(
        functools.partial(_rwkv_chunk_kernel, chunk=chunk),
        grid=(B, T // chunk),
        in_specs=[tok] * 6 + [st],
        out_specs=(tok, st),
        out_shape=(jax.ShapeDtypeStruct((B, T, RWKV_W), F32),
                   jax.ShapeDtypeStruct((B, n_pairs, V7X_LANES, V7X_LANES), F32)),
        scratch_shapes=[pltpu.VMEM((n_pairs, V7X_LANES, V7X_LANES), F32)],
        compiler_params=_cparams("parallel", "arbitrary"),
        name="rwkv_chunk_%d" % chunk,
    )(r, k, v, lw, kk, b, _pair_state(S0))
    return o, _unpair_state(s_out)


def _layer_norm(x, g, b):
    mu = jnp.mean(x, axis=-1, keepdims=True)
    var = jnp.mean(jnp.square(x - mu), axis=-1, keepdims=True)
    return (x - mu) * lax.rsqrt(var + LN_EPS) * g + b


def _mix_out_kernel(ga_ref, gr_ref, att_ref, o_ref, bonus_ref, g_ref, x_ref, wout_ref, gng_ref, gnb_ref,
                    ln_g_ref, ln_b_ref, h32_ref, h16_ref):
    ones_bd = _head_ones(V7X_LANES, RWKV_HD)
    o = o_ref[...]
    mean = _segsum(o, ones_bd) * (1.0 / RWKV_HD)
    d = o - mean
    var = _segsum(d * d, ones_bd) * (1.0 / RWKV_HD)
    rw = d * lax.rsqrt(var + RWKV_GN_EPS) * gng_ref[...] + gnb_ref[...]
    rw = (rw + bonus_ref[...]) * g_ref[...]
    mixed_in = jax.nn.sigmoid(ga_ref[...]) * att_ref[...] + jax.nn.sigmoid(gr_ref[...]) * rw
    mixed = _dot(mixed_in.astype(BF16), wout_ref[...])
    h = _layer_norm(DN_ALPHA * x_ref[...] + mixed, ln_g_ref[...], ln_b_ref[...])
    h32_ref[...] = h
    h16_ref[...] = h.astype(BF16)


def _mix_out(gates, att, o, bonus, g, x2d, wout16, gn_g, gn_b, ln_g, ln_b, tm):
    n = x2d.shape[0]
    row = lambda i: (i, 0)
    const = lambda i: (0, 0)
    tok = pl.BlockSpec((tm, D_MODEL), row)
    vec = pl.BlockSpec((1, D_MODEL), const)
    return pl.pallas_call(
        _mix_out_kernel,
        grid=(n // tm,),
        in_specs=[tok, pl.BlockSpec((tm, D_MODEL), lambda i: (i, 1)), tok, tok, tok, tok, tok,
                  pl.BlockSpec((D_MODEL, D_MODEL), const), vec, vec, vec, vec],
        out_specs=(tok, tok),
        out_shape=(jax.ShapeDtypeStruct((n, D_MODEL), F32), jax.ShapeDtypeStruct((n, D_MODEL), BF16)),
        compiler_params=_cparams("parallel"),
        name="mix_out",
    )(gates, gates, att, o, bonus, g, x2d, wout16, gn_g.reshape(1, -1), gn_b.reshape(1, -1),
      ln_g.reshape(1, -1), ln_b.reshape(1, -1))


_STAIR = tuple((p, q) for p in range(PEER_TOPK) for q in range(PEER_TOPK // (p + 1)))
_STAIR_ROWS = -(-len(_STAIR) // V7X_SUBLANES) * V7X_SUBLANES
_RANK_MARK0 = -3.0e38
_RANK_MARK_STEP = 1.0e36
_RANK_PAD = -2.9e38


def _top_ranks(x, k, exact):
    R = x.shape[0]
    row = lax.broadcasted_iota(jnp.int32, x.shape, 0)
    vals = []
    for it in range(k):
        m = jnp.max(x, axis=0, keepdims=True)
        if exact:
            hit = row == jnp.min(jnp.where(x == m, row, R), axis=0, keepdims=True)
        else:
            hit = x == m
        x = jnp.where(hit, _RANK_MARK0 - it * _RANK_MARK_STEP, x)
        vals.append(m)
    marked = x <= _RANK_MARK0
    rank = jnp.where(marked, jnp.floor((_RANK_MARK0 - x) * (1.0 / _RANK_MARK_STEP) + 0.5), float(k))
    count = jnp.sum(jnp.where(marked, 1.0, 0.0), axis=0, keepdims=True)
    return vals, rank, count


def _peer_select_kernel(h_ref, wq_ref, keys_ref, cut_ref, f1_ref, rank2_ref, e2_ref, cand_scr, sel_scr):
    q = _dot(h_ref[...], wq_ref[...]).astype(BF16)
    K = PEER_TOPK

    def select(exact):
        most = jnp.zeros((1, q.shape[0]), F32)
        for h in range(PEER_HEADS):
            s = []
            for c in range(2):
                blk = (h * 2 + c) * PEER_HALF
                s.append(_dot_nt(keys_ref[h * 2 + c], q[:, blk:blk + PEER_HALF]))
            a1, rank1, n1 = _top_ranks(s[0], K, exact)
            a2, rank2, n2 = _top_ranks(s[1], K, exact)
            cand_scr[...] = jnp.full(cand_scr.shape, _RANK_PAD, F32)
            for i, (p, qq) in enumerate(_STAIR):
                cand_scr[i:i + 1, :] = a1[p] + a2[qq]
            cand = cand_scr[...]
            _, crank, nc = _top_ranks(cand, K, exact)
            most = jnp.maximum(most, jnp.maximum(jnp.maximum(n1, n2), nc))
            chosen = crank < float(K)
            z = jnp.sum(jnp.where(chosen, jnp.exp(cand - (a1[0] + a2[0])), 0.0), axis=0, keepdims=True)
            sel_scr[...] = jnp.where(chosen, 1.0, 0.0)
            cut = jnp.zeros(rank1.shape, F32)
            start = 0
            for p in range(K):
                n_p = K // (p + 1)
                cut_p = jnp.sum(sel_scr[start:start + n_p, :], axis=0, keepdims=True)
                cut = jnp.where(rank1 == float(p), cut_p, cut)
                start += n_p
            cut_ref[h] = cut
            f1_ref[h] = jnp.exp(s[0] - a1[0]) * (0.5 / z)
            e2 = jnp.exp(s[1] - a2[0])
            group = rank2_ref.shape[-1]
            for g in range(rank2.shape[1] // group):
                cols = slice(g * group, (g + 1) * group)
                rank2_ref[h, g] = rank2[:, cols].reshape(_KEY_TILES, V7X_SUBLANES, group).astype(BF16)
                e2_ref[h, g] = e2[:, cols].reshape(_KEY_TILES, V7X_SUBLANES, group).astype(BF16)
        return most

    most = select(exact=False)

    @pl.when(jnp.max(most) > float(K))
    def _():
        select(exact=True)


_KEY_TILES = N_KEYS // V7X_SUBLANES


def _peer_group(n):
    return 2 * V7X_LANES if n % (2 * V7X_LANES) == 0 else V7X_LANES


def _peer_select(h16, wq16, keys16, tp):
    n = h16.shape[0]
    group = _peer_group(tp)
    sel_shape = jax.ShapeDtypeStruct((PEER_HEADS, N_KEYS, n), F32)
    sel_spec = pl.BlockSpec((PEER_HEADS, N_KEYS, tp), lambda i: (0, 0, i))
    tab_shape = jax.ShapeDtypeStruct((PEER_HEADS, n // group, _KEY_TILES, V7X_SUBLANES, group), BF16)
    tab_spec = pl.BlockSpec((PEER_HEADS, tp // group, _KEY_TILES, V7X_SUBLANES, group),
                            lambda i: (0, i, 0, 0, 0))
    return pl.pallas_call(
        _peer_select_kernel,
        grid=(n // tp,),
        in_specs=[pl.BlockSpec((tp, D_MODEL), lambda i: (i, 0)),
                  pl.BlockSpec(wq16.shape, lambda i: (0, 0)),
                  pl.BlockSpec(keys16.shape, lambda i: (0, 0, 0))],
        out_specs=(sel_spec, sel_spec, tab_spec, tab_spec),
        out_shape=(sel_shape, sel_shape, tab_shape, tab_shape),
        scratch_shapes=[pltpu.VMEM((_STAIR_ROWS, tp), F32), pltpu.VMEM((_STAIR_ROWS, tp), F32)],
        compiler_params=_cparams("parallel"),
        name="peer_select",
    )(h16, wq16, keys16)


_PEER_I1_PER_BLOCK = 8
_PEER_I1_PER_CHUNK = 2
_SQRT_HALF = math.sqrt(0.5)


def _peer_dense_kernel(x_ref, u_ref, vt_ref, cut_ref, f1_ref, rank2_ref, e2_ref, h_ref, ln_g_ref, ln_b_ref,
                       y_ref, acc_scr, hid_scr, a_scr):
    e = pl.program_id(1)

    @pl.when(e == 0)
    def _():
        acc_scr[...] = jnp.zeros(acc_scr.shape, F32)
        hid_scr[...] = jnp.zeros(hid_scr.shape, F32)

    group = rank2_ref.shape[-1]
    n_groups = x_ref.shape[0] // group
    tile3 = (_KEY_TILES, V7X_SUBLANES, group)
    assert n_groups <= _PEER_I1_PER_BLOCK // _PEER_I1_PER_CHUNK

    def stages(cur, prev):
        def next_hid(tc):
            hid_scr[cur, tc] = _dot_nt(u_ref[...], x_ref[tc * group:(tc + 1) * group, :])

        for ch in range(_PEER_I1_PER_BLOCK // _PEER_I1_PER_CHUNK):
            if ch < n_groups:
                next_hid(ch)
            for il in range(ch * _PEER_I1_PER_CHUNK, (ch + 1) * _PEER_I1_PER_CHUNK):
                rows = slice(il * N_KEYS, (il + 1) * N_KEYS)
                for tc in range(n_groups):
                    cols = slice(tc * group, (tc + 1) * group)
                    w = jnp.zeros(tile3, BF16)
                    for h in range(PEER_HEADS):
                        cutb = jnp.broadcast_to(cut_ref[h, il:il + 1, cols], tile3[1:]).astype(BF16)
                        f1b = jnp.broadcast_to(f1_ref[h, il:il + 1, cols], tile3[1:]).astype(BF16)
                        e2 = e2_ref[h, tc]
                        w = w + jnp.where(rank2_ref[h, tc] < cutb[None], e2 * f1b[None], jnp.zeros_like(e2))
                    hb = hid_scr[prev, tc, rows, :]
                    act = (hb * (1.0 + lax.erf(hb * _SQRT_HALF))).reshape(tile3).astype(BF16)
                    a_scr[rows, cols] = (w * act).reshape(N_KEYS, group)
            crow = slice(ch * _PEER_I1_PER_CHUNK * N_KEYS, (ch + 1) * _PEER_I1_PER_CHUNK * N_KEYS)
            acc_scr[...] += _dot(vt_ref[:, crow], a_scr[crow, :])

    for parity in range(2):
        @pl.when(e % 2 == parity)
        def _(parity=parity):
            stages(parity, 1 - parity)

    @pl.when(e == pl.num_programs(1) - 1)
    def _():
        ff = acc_scr[...].T
        y_ref[...] = _layer_norm(DN_ALPHA * h_ref[...] + ff, ln_g_ref[...], ln_b_ref[...])


def _peer_dense(h32, h16, u16, vt16, sel, ln_g, ln_b, tt):
    n = h32.shape[0]
    cut, f1, rank2, e2 = sel
    group = rank2.shape[-1]
    assert tt % group == 0
    eb = _PEER_I1_PER_BLOCK * N_KEYS
    n_e = N_EXPERTS // eb
    tok = lambda i, e: (i, 0)
    const = lambda i, e: (0, 0)
    blk = lambda e, lag: jnp.clip(e - lag, 0, n_e - 1)
    tab_spec = pl.BlockSpec((PEER_HEADS, tt // group, _KEY_TILES, V7X_SUBLANES, group), lambda i, e: (0, i, 0, 0, 0))
    return pl.pallas_call(
        _peer_dense_kernel,
        grid=(n // tt, n_e + 1),
        in_specs=[pl.BlockSpec((tt, D_MODEL), tok),
                  pl.BlockSpec((eb, D_MODEL), lambda i, e: (blk(e, 0), 0)),
                  pl.BlockSpec((None, D_MODEL, eb), lambda i, e: (blk(e, 1), 0, 0)),
                  pl.BlockSpec((PEER_HEADS, _PEER_I1_PER_BLOCK, tt), lambda i, e: (0, blk(e, 1), i)),
                  pl.BlockSpec((PEER_HEADS, _PEER_I1_PER_BLOCK, tt), lambda i, e: (0, blk(e, 1), i)),
                  tab_spec, tab_spec,
                  pl.BlockSpec((tt, D_MODEL), tok),
                  pl.BlockSpec((1, D_MODEL), const), pl.BlockSpec((1, D_MODEL), const)],
        out_specs=pl.BlockSpec((tt, D_MODEL), tok),
        out_shape=jax.ShapeDtypeStruct((n, D_MODEL), F32),
        scratch_shapes=[pltpu.VMEM((D_MODEL, tt), F32), pltpu.VMEM((2, tt // group, eb, group), F32),
                        pltpu.VMEM((eb, tt), BF16)],
        compiler_params=_cparams("parallel", "arbitrary"),
        name="peer_dense",
    )(h16, u16, vt16, cut, f1, rank2, e2, h32, ln_g.reshape(1, -1), ln_b.reshape(1, -1))


_TM_PROMPT = 256
_TQ_PROMPT = 512
_RWKV_CHUNK = 64
_RWKV_CHUNK_SAMPLE = 16
_PAGES_PER_STEP = 8
_PEER_TP = 256
_PEER_TT = 512


def _layer_tail(gates, att2, o2, bonus2, g2, x2, lw, tm, tp, tt):
    h32, h16 = _mix_out(gates, att2, o2, bonus2, g2, x2, lw["wout16"], lw["rwkv_ln_g"], lw["rwkv_ln_b"],
                        lw["ln1_g"], lw["ln1_b"], tm)
    sel = _peer_select(h16, lw["wq16"], lw["keys16"], tp)
    return _peer_dense(h32, h16, lw["u16"], lw["vt16"], sel, lw["ln2_g"], lw["ln2_b"], tt)


def _prompt_layer(x, lw, rel_bias):
    B, T, _ = x.shape
    n = B * T
    x2 = x.reshape(n, D_MODEL)
    q16, k32, k16, v32, v16, gates, p = _in_proj(x2, lw["w_in16"], _TM_PROMPT)
    sh = lambda t: t.reshape(B, T, -1)
    vt16 = jnp.swapaxes(sh(v16), 1, 2)
    att = _prompt_attention(sh(q16), sh(k16), vt16, rel_bias, lw["lam_qk"], lw["attn_norm_g"], _TQ_PROMPT)
    p3 = sh(p)
    shift0 = jnp.zeros((B, RWKV_PROJ), F32)
    S0 = jnp.zeros((B, RWKV_HEADS, RWKV_HD, RWKV_HD), F32)
    o, g, bonus, S_new = _rwkv_fused(p3, shift0, lw["rwkv_params"], S0, _RWKV_CHUNK)
    fl = lambda t: t.reshape(n, -1)
    y = _layer_tail(gates, fl(att), fl(o), fl(bonus), fl(g), x2, lw, _TM_PROMPT, _PEER_TP, _PEER_TT)
    return (y.reshape(B, T, D_MODEL), k32.reshape(B, T, ATT_HEADS, 2 * ATT_HD), v32.reshape(B, T, ATT_HEADS, ATT_VD),
            S_new, p3[:, -1, :])


def _sample_layer(x, lw, rel_bias, cache_k, cache_v, layer, page_table, shift0, S0):
    DB, S, _ = x.shape
    n = DB * S
    x2 = x.reshape(n, D_MODEL)
    q16, k32, k16, v32, v16, gates, p = _in_proj(x2, lw["w_in16"], n)
    sh = lambda t: t.reshape(DB, S, -1)
    att = _sample_attention(sh(q16), sh(k32), sh(v32), cache_k, cache_v, layer, page_table, rel_bias,
                            lw["lam_qk"], lw["attn_norm_g"], _PAGES_PER_STEP)
    p3 = sh(p)
    prev = jnp.concatenate([shift0[:, None, :], p3[:, :-1, :]], axis=1).reshape(n, RWKV_PROJ)
    r, k, v, lgw, kk, b, g, bonus = _rwkv_prep_sample(p, prev, lw["rwkv_params"])
    pad = lambda t: jnp.pad(sh(t), ((0, 0), (0, _RWKV_CHUNK_SAMPLE - S), (0, 0)))
    o, S_new = _rwkv_chunked(pad(r), pad(k), pad(v), pad(lgw), pad(kk), pad(b), S0, _RWKV_CHUNK_SAMPLE)
    o2 = o[:, :S, :].reshape(n, RWKV_W)
    y = _layer_tail(gates, att.reshape(n, ATT_V), o2, bonus, g, x2, lw, n, n, n)
    return (y.reshape(DB, S, D_MODEL), k32.reshape(DB, S, ATT_HEADS, 2 * ATT_HD), v32.reshape(DB, S, ATT_HEADS, ATT_VD),
            S_new, p3[:, -1, :])


def kernel(x_prompt, x_sample, cache_k, cache_v, state_wkv, state_shift, page_table, w_in, w_out, lam_qk, attn_norm_g, shift_mu, rwkv_w0, rwkv_w2, rwkv_a0, rwkv_a2, rwkv_g2, rwkv_kk, rwkv_ka, rwkv_rk, rwkv_ln_g, rwkv_ln_b, ln1_g, ln1_b, ln2_g, ln2_b, peer_wq, peer_keys, peer_u, peer_v, rel_bias):
    assert w_in.shape[0] == DEPTH == 1
    yp, ys = x_prompt, x_sample
    outs = [[] for _ in range(8)]
    for l in range(DEPTH):
        lw = {
            "w_in16": w_in[l].astype(BF16), "wout16": w_out[l].astype(BF16),
            "lam_qk": lam_qk[l], "attn_norm_g": attn_norm_g[l],
            "rwkv_params": _rwkv_param_list(shift_mu[l], rwkv_w0[l], rwkv_w2[l], rwkv_a0[l], rwkv_a2[l],
                                            rwkv_g2[l], rwkv_kk[l], rwkv_ka[l], rwkv_rk[l]),
            "rwkv_ln_g": rwkv_ln_g[l], "rwkv_ln_b": rwkv_ln_b[l],
            "ln1_g": ln1_g[l], "ln1_b": ln1_b[l], "ln2_g": ln2_g[l], "ln2_b": ln2_b[l],
            "wq16": peer_wq[l].astype(BF16),
            "keys16": peer_keys[l].reshape(PEER_HEADS * 2, N_KEYS, PEER_HALF).astype(BF16),
            "u16": peer_u[l].astype(BF16),
            "vt16": jnp.swapaxes(peer_v[l].astype(BF16).reshape(-1, _PEER_I1_PER_BLOCK * N_KEYS, D_MODEL), 1, 2),
        }
        yp, kp, vp, Sp, shp = _prompt_layer(yp, lw, rel_bias)
        ys, ksm, vsm, Ss, shs = _sample_layer(ys, lw, rel_bias, cache_k, cache_v, l, page_table,
                                              state_shift[l], state_wkv[l])
        for lst, val in zip(outs, (kp, vp, ksm, vsm, Sp, Ss, shp, shs)):
            lst.append(val)
    stacked = tuple(jnp.stack(lst, axis=0) for lst in outs)
    return (yp, ys) + stacked
```

```python
import functools
import math

import jax
import jax.numpy as jnp
from jax import lax
from jax.experimental import pallas as pl
from jax.experimental.pallas import tpu as pltpu

D_MODEL = 1024
PAGE_SIZE = 128
ATT_HD = 64
ATT_HEADS = D_MODEL // (2 * ATT_HD)
ATT_VD = 2 * ATT_HD
ATT_QK = ATT_HEADS * 2 * ATT_HD
ATT_V = ATT_HEADS * ATT_VD
N_BUCKETS = 32
MAX_DISTANCE = 128
ATT_EPS = 1e-5
RWKV_HD = 64
RWKV_HEADS = D_MODEL // RWKV_HD
RWKV_W = RWKV_HEADS * RWKV_HD
DECAY_LORA = 64
AAA_LORA = 64
GATE_LORA = 128
RWKV_PROJ = 3 * RWKV_W + DECAY_LORA + AAA_LORA + GATE_LORA
DECAY_SCALE = 0.606531
RWKV_GN_EPS = 64e-5
O_K = ATT_QK
O_V = O_K + ATT_QK
O_GA = O_V + ATT_V
O_GR = O_GA + D_MODEL
O_RW = O_GR + D_MODEL
N_IN = O_RW + RWKV_PROJ
PEER_HEADS = 8
N_KEYS = 128
N_EXPERTS = N_KEYS * N_KEYS
PEER_QDIM = 256
PEER_HALF = PEER_QDIM // 2
PEER_TOPK = 16
DEPTH = 1
DN_ALPHA = (2.0 * DEPTH) ** 0.25
LN_EPS = 1e-5
LAMBDA_INIT = 0.8 - 0.6 * math.exp(-0.3 * 0)

V7X_LANES = 128
V7X_SUBLANES = 8
V7X_VMEM_LIMIT_BYTES = 56 * 1024 * 1024

NEG_INF = -1e30
LOG2E = math.log2(math.e)
BF16 = jnp.bfloat16
F32 = jnp.float32

NT_DIMS = (((1,), (1,)), ((), ()))
TN_DIMS = (((0,), (0,)), ((), ()))


def _cparams(*sem, flags=None):
    return pltpu.CompilerParams(dimension_semantics=sem, vmem_limit_bytes=V7X_VMEM_LIMIT_BYTES, flags=flags)


def _dot(a, b):
    return jnp.dot(a, b, preferred_element_type=F32)


def _dot_nt(a, b):
    return lax.dot_general(a, b, NT_DIMS, preferred_element_type=F32)


def _dot_tn(a, b):
    return lax.dot_general(a, b, TN_DIMS, preferred_element_type=F32)


def _split_dot(x, w_bf16):
    hi = x.astype(BF16)
    lo = (x - hi.astype(F32)).astype(BF16)
    return _dot(hi, w_bf16) + _dot(lo, w_bf16)


_P_CHUNKS = ((0, 1024), (1024, 2048), (2048, 3072), (3072, RWKV_PROJ))


def _in_proj_kernel(x_ref, w_ref, q16_ref, k32_ref, k16_ref, v32_ref, v16_ref, gates_ref, p_ref):
    x = x_ref[...].astype(BF16)

    def mm(lo, hi):
        return _dot(x, w_ref[:, lo:hi])

    q16_ref[...] = (mm(0, O_K) * (ATT_HD ** -0.5 * LOG2E)).astype(BF16)
    k = mm(O_K, O_V)
    k32_ref[...] = k
    k16_ref[...] = k.astype(BF16)
    v = mm(O_V, O_GA)
    v32_ref[...] = v
    v16_ref[...] = v.astype(BF16)
    gates_ref[:, 0:D_MODEL] = mm(O_GA, O_GR)
    gates_ref[:, D_MODEL:2 * D_MODEL] = mm(O_GR, O_RW)
    for lo, hi in _P_CHUNKS:
        p_ref[:, lo:hi] = mm(O_RW + lo, O_RW + hi)


def _in_proj(x2d, w_bf16, tm):
    n = x2d.shape[0]
    row = lambda i: (i, 0)
    full = lambda i: (0, 0)
    out_shape = (
        jax.ShapeDtypeStruct((n, ATT_QK), BF16),
        jax.ShapeDtypeStruct((n, ATT_QK), F32),
        jax.ShapeDtypeStruct((n, ATT_QK), BF16),
        jax.ShapeDtypeStruct((n, ATT_V), F32),
        jax.ShapeDtypeStruct((n, ATT_V), BF16),
        jax.ShapeDtypeStruct((n, 2 * D_MODEL), F32),
        jax.ShapeDtypeStruct((n, RWKV_PROJ), F32),
    )
    out_specs = tuple(pl.BlockSpec((tm, s.shape[1]), row) for s in out_shape)
    return pl.pallas_call(
        _in_proj_kernel,
        grid=(n // tm,),
        in_specs=[pl.BlockSpec((tm, D_MODEL), row),
                  pl.BlockSpec((D_MODEL, N_IN), full, pipeline_mode=pl.Buffered(1))],
        out_specs=out_specs,
        out_shape=out_shape,
        compiler_params=_cparams("parallel"),
        name="in_proj",
    )(x2d, w_bf16)


def _t5_bucket(rel):
    n = jnp.maximum(rel, 0)
    max_exact = N_BUCKETS // 2
    nf = jnp.maximum(n, 1).astype(F32)
    large = max_exact + (jnp.log(nf / max_exact) / math.log(MAX_DISTANCE / max_exact)
                         * (N_BUCKETS - max_exact)).astype(jnp.int32)
    large = jnp.minimum(large, N_BUCKETS - 1)
    return jnp.where(n < max_exact, n, large)


def _shifted_bias(rel_bias, dist):
    far = lax.dynamic_index_in_dim(rel_bias, _t5_bucket(jnp.int32(2 * MAX_DISTANCE)), 0, keepdims=False)
    table = (rel_bias - far[None, :]) * LOG2E
    bucket = _t5_bucket(dist)[None]
    expand = (slice(None),) + (None,) * dist.ndim
    out = jnp.zeros((rel_bias.shape[1],) + dist.shape, F32)
    for b in range(N_BUCKETS):
        out = jnp.where(bucket == b, table[b][expand], out)
    return out


def _per_row_head(tiles, row_h):
    out = jnp.zeros(tiles.shape[1:], F32)
    for h in range(tiles.shape[0]):
        out = jnp.where(row_h[:, None] == h, tiles[h], out)
    return out


def _lambda_value(lq):
    s01 = jnp.sum(lq[0:1] * lq[1:2], axis=-1, keepdims=True)
    s23 = jnp.sum(lq[2:3] * lq[3:4], axis=-1, keepdims=True)
    return jnp.exp(s01) - jnp.exp(s23) + LAMBDA_INIT


def _rms_scale(o):
    return o * lax.rsqrt(jnp.mean(o * o, axis=-1, keepdims=True) + ATT_EPS)


def _prompt_attn_kernel(q_ref, k_ref, vt_ref, bias_ref, lam_ref, g_ref, o_ref, m_ref, l_ref, acc_ref,
                        sa_ref, sb_ref, *, tq):
    i = pl.program_id(2)
    tk = tq // 2
    lane = lax.broadcasted_iota(jnp.int32, (tq, 2 * ATT_HD), 1)
    q = q_ref[0]
    zero = jnp.zeros_like(q)
    qm = (jnp.where(lane < ATT_HD, q, zero), jnp.where(lane >= ATT_HD, q, zero))

    m_ref[...] = jnp.full(m_ref.shape, NEG_INF, F32)
    l_ref[...] = jnp.zeros(l_ref.shape, F32)
    acc_ref[...] = jnp.zeros(acc_ref.shape, F32)

    def scores(b, s_ref):
        kb = k_ref[0, pl.ds(pl.multiple_of(b * tk, tk), tk), :]
        for c in range(2):
            s_ref[c] = _dot_nt(kb, qm[c])

    def consume(b, s_ref, add_t):
        vt = vt_ref[0, :, pl.ds(pl.multiple_of(b * tk, tk), tk)]
        p16, alphas = [], []
        for c in range(2):
            sc = s_ref[c] if add_t is None else s_ref[c] + add_t
            m_old = m_ref[c]
            m_new = jnp.maximum(m_old, jnp.max(sc, axis=0, keepdims=True))
            alpha = jnp.exp2(m_old - m_new)
            p = jnp.exp2(sc - m_new)
            l_ref[c] = alpha * l_ref[c] + jnp.sum(p, axis=0, keepdims=True)
            m_ref[c] = m_new
            p16.append(p.astype(BF16))
            alphas.append(alpha)
        for c in range(2):
            acc_ref[c] = alphas[c] * acc_ref[c] + _dot(vt, p16[c])

    scores(0, sa_ref)

    def far_body(p, carry):
        scores(2 * p + 1, sb_ref)
        consume(2 * p, sa_ref, None)
        scores(2 * p + 2, sa_ref)
        consume(2 * p + 1, sb_ref, None)
        return carry

    lax.fori_loop(0, jnp.maximum(i - 1, 0), far_body, 0)

    @pl.when(i > 0)
    def _():
        scores(2 * i - 1, sb_ref)
        consume(2 * i - 2, sa_ref, None)
        scores(2 * i, sa_ref)
        consume(2 * i - 1, sb_ref, bias_ref[0, 0])

    scores(2 * i + 1, sb_ref)
    consume(2 * i, sa_ref, bias_ref[0, 1])
    consume(2 * i + 1, sb_ref, bias_ref[0, 2])

    lam = _lambda_value(lam_ref[...])
    ot = acc_ref[0] / l_ref[0] - lam * (acc_ref[1] / l_ref[1])
    ot = ot * lax.rsqrt(jnp.mean(ot * ot, axis=0, keepdims=True) + ATT_EPS)
    o_ref[0] = ot.T * (g_ref[...] * (1.0 - LAMBDA_INIT))


def _prompt_attention(q16, k16, vt16, rel_bias, lam_qk, attn_norm_g, tq):
    B, T, _ = q16.shape
    tk = tq // 2
    assert tk >= MAX_DISTANCE
    diff_t = jnp.arange(tq, dtype=jnp.int32)[None, :] - jnp.arange(tk, dtype=jnp.int32)[:, None]
    tiles = []
    for off in (tk, 0, -tk):
        dist = diff_t + off
        tiles.append(jnp.where(dist >= 0, _shifted_bias(rel_bias, dist), NEG_INF))
    bias = jnp.stack(tiles, axis=1).astype(F32)
    kern = functools.partial(_prompt_attn_kernel, tq=tq)
    return pl.pallas_call(
        kern,
        grid=(B, ATT_HEADS, T // tq),
        in_specs=[
            pl.BlockSpec((1, tq, 2 * ATT_HD), lambda b, h, i: (b, i, h)),
            pl.BlockSpec((1, T, 2 * ATT_HD), lambda b, h, i: (b, 0, h)),
            pl.BlockSpec((1, ATT_VD, T), lambda b, h, i: (b, h, 0)),
            pl.BlockSpec((1, 3, tk, tq), lambda b, h, i: (h, 0, 0, 0)),
            pl.BlockSpec((4, ATT_HD), lambda b, h, i: (0, 0)),
            pl.BlockSpec((1, ATT_VD), lambda b, h, i: (0, h)),
        ],
        out_specs=pl.BlockSpec((1, tq, ATT_VD), lambda b, h, i: (b, i, h)),
        out_shape=jax.ShapeDtypeStruct((B, T, ATT_V), F32),
        scratch_shapes=[pltpu.VMEM((2, 1, tq), F32), pltpu.VMEM((2, 1, tq), F32),
                        pltpu.VMEM((2, ATT_VD, tq), F32),
                        pltpu.VMEM((2, tk, tq), F32), pltpu.VMEM((2, tk, tq), F32)],
        compiler_params=_cparams("parallel", "parallel", "arbitrary"),
        name="prompt_attention",
    )(q16, k16, vt16, bias, lam_qk, attn_norm_g.reshape(1, ATT_V))


def _sample_attn_kernel(pt_ref, q_ref, plain_ref, last_ref, new_ref, knew_ref, vnew_ref, lam_ref, g_ref,
                        *rest, pages_per_step):
    k_refs = rest[:pages_per_step]
    v_refs = rest[pages_per_step:2 * pages_per_step]
    o_ref, m_ref, l_ref, acc_ref = rest[2 * pages_per_step:]
    j = pl.program_id(1)
    last = pl.num_programs(1) - 1
    q = q_ref[0]

    @pl.when(j == 0)
    def _():
        m_ref[...] = jnp.full(m_ref.shape, NEG_INF, F32)
        l_ref[...] = jnp.zeros(l_ref.shape, F32)
        acc_ref[...] = jnp.zeros(acc_ref.shape, F32)

    def update(s, v2d):
        m_old = m_ref[...]
        m_new = jnp.maximum(m_old, jnp.max(s, axis=-1, keepdims=True))
        alpha = jnp.exp2(m_old - m_new)
        p = jnp.exp2(s - m_new)
        l_ref[...] = alpha * l_ref[...] + jnp.sum(p, axis=-1, keepdims=True)
        acc_ref[...] = alpha * acc_ref[...] + _dot(p.astype(BF16), v2d)
        m_ref[...] = m_new

    rows2d = PAGE_SIZE * ATT_HEADS
    k_all = jnp.concatenate([r[...].reshape(rows2d, 2 * ATT_HD).astype(BF16) for r in k_refs], axis=0)
    v_all = jnp.concatenate([r[...].reshape(rows2d, ATT_VD).astype(BF16) for r in v_refs], axis=0)
    add = jnp.concatenate([plain_ref[...]] * (pages_per_step - 1) + [last_ref[0]], axis=1)
    update(_dot_nt(q, k_all) + add, v_all)

    @pl.when(j == last)
    def _():
        n_new = knew_ref.shape[1]
        k2d = knew_ref[0].reshape(n_new * ATT_HEADS, 2 * ATT_HD).astype(BF16)
        v2d = vnew_ref[0].reshape(n_new * ATT_HEADS, ATT_VD).astype(BF16)
        update(_dot_nt(q, k2d) + new_ref[...], v2d)
        half = acc_ref.shape[0] // 2
        o = acc_ref[...] / l_ref[...]
        lam = _lambda_value(lam_ref[...])
        diff = o[0:half] - lam * o[half:2 * half]
        o_ref[0] = _rms_scale(diff) * (g_ref[...] * (1.0 - LAMBDA_INIT))


def _sample_attention(q16, k32, v32, cache_k, cache_v, layer, page_table, rel_bias, lam_qk, attn_norm_g,
                      pages_per_step):
    DB, S, _ = q16.shape
    n_pages = page_table.shape[1]
    past = n_pages * PAGE_SIZE
    H = ATT_HEADS
    rows = 2 * S * H
    q5 = q16.reshape(DB, S, H, 2, ATT_HD)
    qrows = jnp.stack([jnp.concatenate([q5[:, :, :, 0], jnp.zeros_like(q5[:, :, :, 0])], axis=-1),
                       jnp.concatenate([jnp.zeros_like(q5[:, :, :, 1]), q5[:, :, :, 1]], axis=-1)], axis=1)
    qrows = qrows.reshape(DB, rows, 2 * ATT_HD)
    row_s = (jnp.arange(rows, dtype=jnp.int32) // H) % S
    row_h = jnp.arange(rows, dtype=jnp.int32) % H
    col_t = jnp.arange(PAGE_SIZE * H, dtype=jnp.int32) // H
    col_h = jnp.arange(PAGE_SIZE * H, dtype=jnp.int32) % H
    same_head = row_h[:, None] == col_h[None, :]
    plain = jnp.where(same_head, 0.0, NEG_INF).astype(F32)
    dist_last = PAGE_SIZE + row_s[:, None] - col_t[None, :]
    b_last = _per_row_head(_shifted_bias(rel_bias, dist_last), row_h)
    last_tile = jnp.where(same_head, b_last, NEG_INF).astype(F32)
    tiles = jnp.stack([plain, last_tile], axis=0)
    ncol_t = jnp.arange(S * H, dtype=jnp.int32) // H
    ncol_h = jnp.arange(S * H, dtype=jnp.int32) % H
    dist_new = row_s[:, None] - ncol_t[None, :]
    b_new = _per_row_head(_shifted_bias(rel_bias, dist_new), row_h)
    ok_new = (row_h[:, None] == ncol_h[None, :]) & (dist_new >= 0)
    new_tile = jnp.where(ok_new, b_new, NEG_INF).astype(F32)
    g_tile = jnp.tile(attn_norm_g.reshape(H, ATT_VD), (S, 1))

    assert n_pages % pages_per_step == 0
    n_steps = n_pages // pages_per_step
    kern = functools.partial(_sample_attn_kernel, pages_per_step=pages_per_step)

    def page_spec(g):
        return pl.BlockSpec((None, None, PAGE_SIZE, H, 2 * ATT_HD),
                            lambda b, j, pt, g=g: (layer, pt[b, j * pages_per_step + g], 0, 0, 0))

    grid_spec = pltpu.PrefetchScalarGridSpec(
        num_scalar_prefetch=1,
        grid=(DB, n_steps),
        in_specs=[
            pl.BlockSpec((1, rows, 2 * ATT_HD), lambda b, j, pt: (b, 0, 0)),
            pl.BlockSpec((None, rows, PAGE_SIZE * H), lambda b, j, pt: (0, 0, 0)),
            pl.BlockSpec((1, rows, PAGE_SIZE * H), lambda b, j, pt: (jnp.where(j == n_steps - 1, 1, 0), 0, 0)),
            pl.BlockSpec((rows, S * H), lambda b, j, pt: (0, 0)),
            pl.BlockSpec((1, S, H, 2 * ATT_HD), lambda b, j, pt: (b, 0, 0, 0)),
            pl.BlockSpec((1, S, H, ATT_VD), lambda b, j, pt: (b, 0, 0, 0)),
            pl.BlockSpec((4, ATT_HD), lambda b, j, pt: (0, 0)),
            pl.BlockSpec((S * H, ATT_VD), lambda b, j, pt: (0, 0)),
        ] + [page_spec(g) for g in range(pages_per_step)] + [page_spec(g) for g in range(pages_per_step)],
        out_specs=pl.BlockSpec((1, S * H, ATT_VD), lambda b, j, pt: (b, 0, 0)),
        scratch_shapes=[pltpu.VMEM((rows, 1), F32), pltpu.VMEM((rows, 1), F32), pltpu.VMEM((rows, ATT_VD), F32)],
    )
    out = pl.pallas_call(
        kern,
        grid_spec=grid_spec,
        out_shape=jax.ShapeDtypeStruct((DB, S * H, ATT_VD), F32),
        compiler_params=_cparams("parallel", "arbitrary"),
        name="sample_attention",
    )(page_table, qrows, tiles, tiles, new_tile,
      k32.reshape(DB, S, H, 2 * ATT_HD), v32.reshape(DB, S, H, ATT_VD), lam_qk, g_tile,
      *([cache_k] * pages_per_step), *([cache_v] * pages_per_step))
    return out.reshape(DB, S, ATT_V)


def _head_ones(n, seg):
    r = lax.broadcasted_iota(jnp.int32, (n, n), 0) // seg
    c = lax.broadcasted_iota(jnp.int32, (n, n), 1) // seg
    return jnp.where(r == c, 1.0, 0.0).astype(BF16)


def _segsum(x, ones_bd):
    outs = []
    for g in range(x.shape[1] // V7X_LANES):
        outs.append(_split_dot(x[:, g * V7X_LANES:(g + 1) * V7X_LANES], ones_bd))
    return jnp.concatenate(outs, axis=1)


def _rwkv_prep_core(p, prev, mu_ref, w0_ref, w2_ref, a0_ref, a2_ref, g2_ref, kks_ref, ka_ref, rk_ref, outs):
    r_ref, k_ref, v_ref, lw_ref, kk_ref, b_ref, g_ref, bonus_ref = outs
    W = RWKV_W
    xs = p + (prev - p) * mu_ref[...]
    r = xs[:, 0:W]
    k = xs[:, W:2 * W]
    v = xs[:, 2 * W:3 * W]
    wa = xs[:, 3 * W:3 * W + DECAY_LORA + AAA_LORA]
    gd = xs[:, 3 * W + DECAY_LORA + AAA_LORA:RWKV_PROJ]
    logw = -DECAY_SCALE * jax.nn.sigmoid(w0_ref[...] + _dot(jnp.tanh(wa).astype(BF16), w2_ref[...]))
    a = jax.nn.sigmoid(a0_ref[...] + _dot(wa.astype(BF16), a2_ref[...]))
    g = _dot(jax.nn.sigmoid(gd).astype(BF16), g2_ref[...])
    ones_bd = _head_ones(V7X_LANES, RWKV_HD)
    kk = k * kks_ref[...]
    kk = kk / jnp.maximum(jnp.sqrt(_segsum(kk * kk, ones_bd)), 1e-12)
    k = k * (1.0 + (a - 1.0) * ka_ref[...])
    r_ref[...] = r
    k_ref[...] = k
    v_ref[...] = v
    lw_ref[...] = logw
    kk_ref[...] = kk
    b_ref[...] = kk * a
    g_ref[...] = g
    bonus_ref[...] = _segsum(r * k * rk_ref[...], ones_bd) * v


def _rwkv_prep_sample_kernel(p_ref, prev_ref, *rest):
    params, outs = rest[:9], rest[9:]
    _rwkv_prep_core(p_ref[...], prev_ref[...], *params, outs)


def _rwkv_param_list(shift_mu, rwkv_w0, rwkv_w2, rwkv_a0, rwkv_a2, rwkv_g2, rwkv_kk, rwkv_ka, rwkv_rk):
    zeros = jnp.zeros((AAA_LORA, RWKV_W), F32)
    w2p = jnp.concatenate([rwkv_w2, zeros], axis=0).astype(BF16)
    a2p = jnp.concatenate([jnp.zeros((DECAY_LORA, RWKV_W), F32), rwkv_a2], axis=0).astype(BF16)
    row = lambda t: t.reshape(1, -1)
    return [row(shift_mu), row(rwkv_w0), w2p, row(rwkv_a0), a2p, rwkv_g2.astype(BF16),
            row(rwkv_kk), row(rwkv_ka), row(rwkv_rk)]


def _rwkv_prep_sample(p2, prev2, params):
    n = p2.shape[0]
    const = lambda i: (0, 0)
    in_specs = [pl.BlockSpec((n, RWKV_PROJ), const), pl.BlockSpec((n, RWKV_PROJ), const)] + [
        pl.BlockSpec(t.shape, const) for t in params]
    out_shape = tuple(jax.ShapeDtypeStruct((n, RWKV_W), F32) for _ in range(8))
    out_specs = tuple(pl.BlockSpec((n, RWKV_W), const) for _ in range(8))
    return pl.pallas_call(
        _rwkv_prep_sample_kernel,
        grid=(1,),
        in_specs=in_specs, out_specs=out_specs, out_shape=out_shape,
        compiler_params=_cparams("arbitrary"),
        name="rwkv_prep_sample",
    )(p2, prev2, *params)


def _rwkv_chunk_kernel(r_ref, k_ref, v_ref, lw_ref, kk_ref, b_ref, s0_ref, o_ref, sout_ref, s_scr, *, chunk):
    _rwkv_chunk_core(r_ref[0], k_ref[0], v_ref[0], lw_ref[0], kk_ref[0], b_ref[0], s0_ref, o_ref, sout_ref, s_scr,
                     chunk)


def _rwkv_fused_kernel(p_ref, tail_ref, shift_ref, *rest, chunk):
    params, rest = rest[:9], rest[9:]
    s0_ref, o_ref, g_ref, bonus_ref, sout_ref, s_scr = rest[:6]
    r_s, k_s, v_s, lw_s, kk_s, b_s = rest[6:]
    c_idx = pl.program_id(1)
    p = p_ref[0]
    first = jnp.where(c_idx == 0, shift_ref[0], tail_ref[0, V7X_SUBLANES - 1:V7X_SUBLANES, :])
    row = lax.broadcasted_iota(jnp.int32, p.shape, 0)
    prev = jnp.where(row == 0, first, pltpu.roll(p, 1, 0))
    _rwkv_prep_core(p, prev, *params, [r_s, k_s, v_s, lw_s, kk_s, b_s, g_ref.at[0], bonus_ref.at[0]])
    _rwkv_chunk_core(r_s[...], k_s[...], v_s[...], lw_s[...], kk_s[...], b_s[...], s0_ref, o_ref, sout_ref, s_scr,
                     chunk)


def _rwkv_chunk_core(r_in, k_in, v_in, lw, kk_in, b_in, s0_ref, o_ref, sout_ref, s_scr, chunk):
    C = chunk
    c_idx = pl.program_id(1)
    n_pairs = RWKV_HEADS // 2
    L = V7X_LANES

    @pl.when(c_idx == 0)
    def _():
        s_scr[...] = s0_ref[0]

    ti = lax.broadcasted_iota(jnp.int32, (C, C), 0)
    si = lax.broadcasted_iota(jnp.int32, (C, C), 1)
    tri = jnp.where(si <= ti, 1.0, 0.0).astype(BF16)
    hi = lw.astype(BF16)
    rem = lw - hi.astype(F32)
    mid = rem.astype(BF16)
    lo = (rem - mid.astype(F32)).astype(BF16)
    cs = _dot(tri, hi) + _dot(tri, mid) + _dot(tri, lo)
    g = jnp.exp(cs)
    gprev = jnp.exp(cs - lw)
    ginv = jnp.exp(-cs)
    kap_all = kk_in * gprev
    rt_all = r_in * g
    kt_all = k_in * ginv
    bt_all = b_in * ginv
    v_all = v_in
    g_last = g[C - 1:C, :]

    lane = lax.broadcasted_iota(jnp.int32, (C, L), 1)
    in_h0 = lane < RWKV_HD

    def stack(x):
        z = jnp.zeros_like(x)
        return jnp.concatenate([jnp.where(in_h0, x, z), jnp.where(in_h0, z, x)], axis=0)

    rr = lax.broadcasted_iota(jnp.int32, (2 * C, 2 * C), 0)
    cc = lax.broadcasted_iota(jnp.int32, (2 * C, 2 * C), 1)
    same = (rr // C) == (cc // C)
    strict = same & (cc < rr)
    incl = same & (cc <= rr)
    eye = jnp.where(rr == cc, 1.0, 0.0).astype(F32)
    n_sq = max(int(math.ceil(math.log2(C))) - 1, 0)

    prs = range(n_pairs)
    sls = [slice(pr * L, (pr + 1) * L) for pr in prs]
    vs16 = [stack(v_all[:, sl]).astype(BF16) for sl in sls]
    lhs = [jnp.concatenate([stack(kap_all[:, sl]).astype(BF16), stack(rt_all[:, sl]).astype(BF16)], axis=0)
           for sl in sls]
    rhs = [jnp.concatenate([stack(kt_all[:, sl]).astype(BF16), stack(bt_all[:, sl]).astype(BF16)], axis=0)
           for sl in sls]
    A = [_dot_nt(lhs[pr], rhs[pr]) for pr in prs]
    S = [s_scr[pr] for pr in prs]
    KS = [_dot_nt(lhs[pr], S[pr].astype(BF16)) for pr in prs]
    zero = jnp.zeros((2 * C, 2 * C), F32)
    Lk16 = [jnp.where(strict, A[pr][0:2 * C, 0:2 * C], zero).astype(BF16) for pr in prs]
    Lb = [jnp.where(strict, A[pr][0:2 * C, 2 * C:4 * C], zero) for pr in prs]
    MkMb = [jnp.concatenate([jnp.where(incl, A[pr][2 * C:4 * C, 0:2 * C], zero),
                             -jnp.where(incl, A[pr][2 * C:4 * C, 2 * C:4 * C], zero)], axis=1).astype(BF16)
            for pr in prs]
    LkV = [_dot(Lk16[pr], vs16[pr]) for pr in prs]
    X = [eye - Lb[pr] for pr in prs]
    P16 = [Lb[pr].astype(BF16) for pr in prs]
    for _ in range(n_sq):
        P16 = [_dot(P16[pr], P16[pr]).astype(BF16) for pr in prs]
        X = [X[pr] + _dot(X[pr].astype(BF16), P16[pr]) for pr in prs]
    U16 = [_dot(X[pr].astype(BF16), (KS[pr][0:2 * C] + LkV[pr]).astype(BF16)).astype(BF16) for pr in prs]
    O = [KS[pr][2 * C:4 * C] + _dot(MkMb[pr], jnp.concatenate([vs16[pr], U16[pr]], axis=0)) for pr in prs]
    upd = [_dot_tn(jnp.concatenate([vs16[pr], -U16[pr]], axis=0), rhs[pr]) for pr in prs]
    for pr in prs:
        o_ref[0, :, sls[pr]] = O[pr][0:C] + O[pr][C:2 * C]
        s_scr[pr] = (S[pr] + upd[pr]) * g_last[:, sls[pr]]

    @pl.when(c_idx == pl.num_programs(1) - 1)
    def _():
        sout_ref[0] = s_scr[...]


def _pair_state(S):
    B = S.shape[0]
    S = S.reshape(B, RWKV_HEADS // 2, 2, RWKV_HD, RWKV_HD)
    z = jnp.zeros_like(S[:, :, 0])
    top = jnp.concatenate([S[:, :, 0], z], axis=-1)
    bot = jnp.concatenate([z, S[:, :, 1]], axis=-1)
    return jnp.concatenate([top, bot], axis=-2)


def _unpair_state(Sbd):
    B = Sbd.shape[0]
    a = Sbd[:, :, :RWKV_HD, :RWKV_HD]
    b = Sbd[:, :, RWKV_HD:, RWKV_HD:]
    return jnp.stack([a, b], axis=2).reshape(B, RWKV_HEADS, RWKV_HD, RWKV_HD)


def _rwkv_fused(p3, shift0, params, S0, chunk):
    B, T, _ = p3.shape
    n_pairs = RWKV_HEADS // 2
    tail_blocks = chunk // V7X_SUBLANES
    const = lambda bb, c: (0, 0)
    tok = pl.BlockSpec((1, chunk, RWKV_W), lambda bb, c: (bb, c, 0))
    st = pl.BlockSpec((1, n_pairs, V7X_LANES, V7X_LANES), lambda bb, c: (bb, 0, 0, 0))
    in_specs = [
        pl.BlockSpec((1, chunk, RWKV_PROJ), lambda bb, c: (bb, c, 0)),
        pl.BlockSpec((1, V7X_SUBLANES, RWKV_PROJ), lambda bb, c: (bb, jnp.maximum(c * tail_blocks - 1, 0), 0)),
        pl.BlockSpec((1, 1, RWKV_PROJ), lambda bb, c: (bb, 0, 0)),
    ] + [pl.BlockSpec(t.shape, const) for t in params] + [st]
    tok_shape = jax.ShapeDtypeStruct((B, T, RWKV_W), F32)
    o, g, bonus, s_out = pl.pallas_call(
        functools.partial(_rwkv_fused_kernel, chunk=chunk),
        grid=(B, T // chunk),
        in_specs=in_specs,
        out_specs=(tok, tok, tok, st),
        out_shape=(tok_shape, tok_shape, tok_shape,
                   jax.ShapeDtypeStruct((B, n_pairs, V7X_LANES, V7X_LANES), F32)),
        scratch_shapes=[pltpu.VMEM((n_pairs, V7X_LANES, V7X_LANES), F32)] + [pltpu.VMEM((chunk, RWKV_W), F32)] * 6,
        compiler_params=_cparams("parallel", "arbitrary"),
        name="rwkv_fused_%d" % chunk,
    )(p3, p3, shift0.reshape(B, 1, RWKV_PROJ), *params, _pair_state(S0))
    return o, g, bonus, _unpair_state(s_out)


def _rwkv_chunked(r, k, v, lw, kk, b, S0, chunk):
    B, T, _ = r.shape
    n_pairs = RWKV_HEADS // 2
    tok = pl.BlockSpec((1, chunk, RWKV_W), lambda bb, c: (bb, c, 0))
    st = pl.BlockSpec((1, n_pairs, V7X_LANES, V7X_LANES), lambda bb, c: (bb, 0, 0, 0))
    o, s_out = pl.pallas_call(
        functools.partial(_rwkv_chunk_kernel, chunk=chunk),
        grid=(B, T // chunk),
        in_specs=[tok] * 6 + [st],
        out_specs=(tok, st),
        out_shape=(jax.ShapeDtypeStruct((B, T, RWKV_W), F32),
                   jax.ShapeDtypeStruct((B, n_pairs, V7X_LANES, V7X_LANES), F32)),
        scratch_shapes=[pltpu.VMEM((n_pairs, V7X_LANES, V7X_LANES), F32)],
        compiler_params=_cparams("parallel", "arbitrary"),
        name="rwkv_chunk_%d" % chunk,
    )(r, k, v, lw, kk, b, _pair_state(S0))
    return o, _unpair_state(s_out)


def _layer_norm(x, g, b):
    mu = jnp.mean(x, axis=-1, keepdims=True)
    var = jnp.mean(jnp.square(x - mu), axis=-1, keepdims=True)
    return (x - mu) * lax.rsqrt(var + LN_EPS) * g + b


def _mix_out_kernel(ga_ref, gr_ref, att_ref, o_ref, bonus_ref, g_ref, x_ref, wout_ref, gng_ref, gnb_ref,
                    ln_g_ref, ln_b_ref, h32_ref, h16_ref):
    ones_bd = _head_ones(V7X_LANES, RWKV_HD)
    o = o_ref[...]
    mean = _segsum(o, ones_bd) * (1.0 / RWKV_HD)
    d = o - mean
    var = _segsum(d * d, ones_bd) * (1.0 / RWKV_HD)
    rw = d * lax.rsqrt(var + RWKV_GN_EPS) * gng_ref[...] + gnb_ref[...]
    rw = (rw + bonus_ref[...]) * g_ref[...]
    mixed_in = jax.nn.sigmoid(ga_ref[...]) * att_ref[...] + jax.nn.sigmoid(gr_ref[...]) * rw
    mixed = _dot(mixed_in.astype(BF16), wout_ref[...])
    h = _layer_norm(DN_ALPHA * x_ref[...] + mixed, ln_g_ref[...], ln_b_ref[...])
    h32_ref[...] = h
    h16_ref[...] = h.astype(BF16)


def _mix_out(gates, att, o, bonus, g, x2d, wout16, gn_g, gn_b, ln_g, ln_b, tm):
    n = x2d.shape[0]
    row = lambda i: (i, 0)
    const = lambda i: (0, 0)
    tok = pl.BlockSpec((tm, D_MODEL), row)
    vec = pl.BlockSpec((1, D_MODEL), const)
    return pl.pallas_call(
        _mix_out_kernel,
        grid=(n // tm,),
        in_specs=[tok, pl.BlockSpec((tm, D_MODEL), lambda i: (i, 1)), tok, tok, tok, tok, tok,
                  pl.BlockSpec((D_MODEL, D_MODEL), const), vec, vec, vec, vec],
        out_specs=(tok, tok),
        out_shape=(jax.ShapeDtypeStruct((n, D_MODEL), F32), jax.ShapeDtypeStruct((n, D_MODEL), BF16)),
        compiler_params=_cparams("parallel"),
        name="mix_out",
    )(gates, gates, att, o, bonus, g, x2d, wout16, gn_g.reshape(1, -1), gn_b.reshape(1, -1),
      ln_g.reshape(1, -1), ln_b.reshape(1, -1))


_STAIR = tuple((p, q) for p in range(PEER_TOPK) for q in range(PEER_TOPK // (p + 1)))
_STAIR_ROWS = -(-len(_STAIR) // V7X_SUBLANES) * V7X_SUBLANES
_RANK_MARK0 = -3.0e38
_RANK_MARK_STEP = 1.0e36
_RANK_PAD = -2.9e38


def _top_ranks(x, k, exact):
    R = x.shape[0]
    row = lax.broadcasted_iota(jnp.int32, x.shape, 0)
    vals = []
    for it in range(k):
        m = jnp.max(x, axis=0, keepdims=True)
        if exact:
            hit = row == jnp.min(jnp.where(x == m, row, R), axis=0, keepdims=True)
        else:
            hit = x == m
        x = jnp.where(hit, _RANK_MARK0 - it * _RANK_MARK_STEP, x)
        vals.append(m)
    marked = x <= _RANK_MARK0
    rank = jnp.where(marked, jnp.floor((_RANK_MARK0 - x) * (1.0 / _RANK_MARK_STEP) + 0.5), float(k))
    count = jnp.sum(jnp.where(marked, 1.0, 0.0), axis=0, keepdims=True)
    return vals, rank, count


def _peer_select_kernel(h_ref, wq_ref, keys_ref, cut_ref, f1_ref, rank2_ref, e2_ref, cand_scr, sel_scr):
    q = _dot(h_ref[...], wq_ref[...]).astype(BF16)
    K = PEER_TOPK

    def select(exact):
        most = jnp.zeros((1, q.shape[0]), F32)
        for h in range(PEER_HEADS):
            s = []
            for c in range(2):
                blk = (h * 2 + c) * PEER_HALF
                s.append(_dot_nt(keys_ref[h * 2 + c], q[:, blk:blk + PEER_HALF]))
            a1, rank1, n1 = _top_ranks(s[0], K, exact)
            a2, rank2, n2 = _top_ranks(s[1], K, exact)
            cand_scr[...] = jnp.full(cand_scr.shape, _RANK_PAD, F32)
            for i, (p, qq) in enumerate(_STAIR):
                cand_scr[i:i + 1, :] = a1[p] + a2[qq]
            cand = cand_scr[...]
            _, crank, nc = _top_ranks(cand, K, exact)
            most = jnp.maximum(most, jnp.maximum(jnp.maximum(n1, n2), nc))
            chosen = crank < float(K)
            z = jnp.sum(jnp.where(chosen, jnp.exp(cand - (a1[0] + a2[0])), 0.0), axis=0, keepdims=True)
            sel_scr[...] = jnp.where(chosen, 1.0, 0.0)
            cut = jnp.zeros(rank1.shape, F32)
            start = 0
            for p in range(K):
                n_p = K // (p + 1)
                cut_p = jnp.sum(sel_scr[start:start + n_p, :], axis=0, keepdims=True)
                cut = jnp.where(rank1 == float(p), cut_p, cut)
                start += n_p
            cut_ref[h] = cut
            f1_ref[h] = jnp.exp(s[0] - a1[0]) * (0.5 / z)
            e2 = jnp.exp(s[1] - a2[0])
            group = rank2_ref.shape[-1]
            for g in range(rank2.shape[1] // group):
                cols = slice(g * group, (g + 1) * group)
                rank2_ref[h, g] = rank2[:, cols].reshape(_KEY_TILES, V7X_SUBLANES, group).astype(BF16)
                e2_ref[h, g] = e2[:, cols].reshape(_KEY_TILES, V7X_SUBLANES, group).astype(BF16)
        return most

    most = select(exact=False)

    @pl.when(jnp.max(most) > float(K))
    def _():
        select(exact=True)


_KEY_TILES = N_KEYS // V7X_SUBLANES


def _peer_group(n):
    return 2 * V7X_LANES if n % (2 * V7X_LANES) == 0 else V7X_LANES


def _peer_select(h16, wq16, keys16, tp):
    n = h16.shape[0]
    group = _peer_group(tp)
    sel_shape = jax.ShapeDtypeStruct((PEER_HEADS, N_KEYS, n), F32)
    sel_spec = pl.BlockSpec((PEER_HEADS, N_KEYS, tp), lambda i: (0, 0, i))
    tab_shape = jax.ShapeDtypeStruct((PEER_HEADS, n // group, _KEY_TILES, V7X_SUBLANES, group), BF16)
    tab_spec = pl.BlockSpec((PEER_HEADS, tp // group, _KEY_TILES, V7X_SUBLANES, group),
                            lambda i: (0, i, 0, 0, 0))
    return pl.pallas_call(
        _peer_select_kernel,
        grid=(n // tp,),
        in_specs=[pl.BlockSpec((tp, D_MODEL), lambda i: (i, 0)),
                  pl.BlockSpec(wq16.shape, lambda i: (0, 0)),
                  pl.BlockSpec(keys16.shape, lambda i: (0, 0, 0))],
        out_specs=(sel_spec, sel_spec, tab_spec, tab_spec),
        out_shape=(sel_shape, sel_shape, tab_shape, tab_shape),
        scratch_shapes=[pltpu.VMEM((_STAIR_ROWS, tp), F32), pltpu.VMEM((_STAIR_ROWS, tp), F32)],
        compiler_params=_cparams("parallel"),
        name="peer_select",
    )(h16, wq16, keys16)


_PEER_I1_PER_BLOCK = 8
_PEER_I1_PER_CHUNK = 2
_SQRT_HALF = math.sqrt(0.5)


def _peer_dense_kernel(x_ref, u_ref, vt_ref, cut_ref, f1_ref, rank2_ref, e2_ref, h_ref, ln_g_ref, ln_b_ref,
                       y_ref, acc_scr, hid_scr, a_scr):
    e = pl.program_id(1)

    @pl.when(e == 0)
    def _():
        acc_scr[...] = jnp.zeros(acc_scr.shape, F32)
        hid_scr[...] = jnp.zeros(hid_scr.shape, F32)

    group = rank2_ref.shape[-1]
    n_groups = x_ref.shape[0] // group
    tile3 = (_KEY_TILES, V7X_SUBLANES, group)
    assert n_groups <= _PEER_I1_PER_BLOCK // _PEER_I1_PER_CHUNK

    def stages(cur, prev):
        def next_hid(tc):
            hid_scr[cur, tc] = _dot_nt(u_ref[...], x_ref[tc * group:(tc + 1) * group, :])

        for ch in range(_PEER_I1_PER_BLOCK // _PEER_I1_PER_CHUNK):
            if ch < n_groups:
                next_hid(ch)
            for il in range(ch * _PEER_I1_PER_CHUNK, (ch + 1) * _PEER_I1_PER_CHUNK):
                rows = slice(il * N_KEYS, (il + 1) * N_KEYS)
                for tc in range(n_groups):
                    cols = slice(tc * group, (tc + 1) * group)
                    w = jnp.zeros(tile3, BF16)
                    for h in range(PEER_HEADS):
                        cutb = jnp.broadcast_to(cut_ref[h, il:il + 1, cols], tile3[1:]).astype(BF16)
                        f1b = jnp.broadcast_to(f1_ref[h, il:il + 1, cols], tile3[1:]).astype(BF16)
                        e2 = e2_ref[h, tc]
                        w = w + jnp.where(rank2_ref[h, tc] < cutb[None], e2 * f1b[None], jnp.zeros_like(e2))
                    hb = hid_scr[prev, tc, rows, :]
                    act = (hb * (1.0 + lax.erf(hb * _SQRT_HALF))).reshape(tile3).astype(BF16)
                    a_scr[rows, cols] = (w * act).reshape(N_KEYS, group)
            crow = slice(ch * _PEER_I1_PER_CHUNK * N_KEYS, (ch + 1) * _PEER_I1_PER_CHUNK * N_KEYS)
            acc_scr[...] += _dot(vt_ref[:, crow], a_scr[crow, :])

    for parity in range(2):
        @pl.when(e % 2 == parity)
        def _(parity=parity):
            stages(parity, 1 - parity)

    @pl.when(e == pl.num_programs(1) - 1)
    def _():
        ff = acc_scr[...].T
        y_ref[...] = _layer_norm(DN_ALPHA * h_ref[...] + ff, ln_g_ref[...], ln_b_ref[...])


def _peer_dense(h32, h16, u16, vt16, sel, ln_g, ln_b, tt):
    n = h32.shape[0]
    cut, f1, rank2, e2 = sel
    group = rank2.shape[-1]
    assert tt % group == 0
    eb = _PEER_I1_PER_BLOCK * N_KEYS
    n_e = N_EXPERTS // eb
    tok = lambda i, e: (i, 0)
    const = lambda i, e: (0, 0)
    blk = lambda e, lag: jnp.clip(e - lag, 0, n_e - 1)
    tab_spec = pl.BlockSpec((PEER_HEADS, tt // group, _KEY_TILES, V7X_SUBLANES, group), lambda i, e: (0, i, 0, 0, 0))
    return pl.pallas_call(
        _peer_dense_kernel,
        grid=(n // tt, n_e + 1),
        in_specs=[pl.BlockSpec((tt, D_MODEL), tok),
                  pl.BlockSpec((eb, D_MODEL), lambda i, e: (blk(e, 0), 0)),
                  pl.BlockSpec((None, D_MODEL, eb), lambda i, e: (blk(e, 1), 0, 0)),
                  pl.BlockSpec((PEER_HEADS, _PEER_I1_PER_BLOCK, tt), lambda i, e: (0, blk(e, 1), i)),
                  pl.BlockSpec((PEER_HEADS, _PEER_I1_PER_BLOCK, tt), lambda i, e: (0, blk(e, 1), i)),
                  tab_spec, tab_spec,
                  pl.BlockSpec((tt, D_MODEL), tok),
                  pl.BlockSpec((1, D_MODEL), const), pl.BlockSpec((1, D_MODEL), const)],
        out_specs=pl.BlockSpec((tt, D_MODEL), tok),
        out_shape=jax.ShapeDtypeStruct((n, D_MODEL), F32),
        scratch_shapes=[pltpu.VMEM((D_MODEL, tt), F32), pltpu.VMEM((2, tt // group, eb, group), F32),
                        pltpu.VMEM((eb, tt), BF16)],
        compiler_params=_cparams("parallel", "arbitrary"),
        name="peer_dense",
    )(h16, u16, vt16, cut, f1, rank2, e2, h32, ln_g.reshape(1, -1), ln_b.reshape(1, -1))


_TM_PROMPT = 256
_TQ_PROMPT = 512
_RWKV_CHUNK = 64
_RWKV_CHUNK_SAMPLE = 16
_PAGES_PER_STEP = 16
_PEER_TP = 256
_PEER_TT = 512


def _layer_tail(gates, att2, o2, bonus2, g2, x2, lw, tm, tp, tt):
    h32, h16 = _mix_out(gates, att2, o2, bonus2, g2, x2, lw["wout16"], lw["rwkv_ln_g"], lw["rwkv_ln_b"],
                        lw["ln1_g"], lw["ln1_b"], tm)
    sel = _peer_select(h16, lw["wq16"], lw["keys16"], tp)
    return _peer_dense(h32, h16, lw["u16"], lw["vt16"], sel, lw["ln2_g"], lw["ln2_b"], tt)


def _prompt_layer(x, lw, rel_bias):
    B, T, _ = x.shape
    n = B * T
    x2 = x.reshape(n, D_MODEL)
    q16, k32, k16, v32, v16, gates, p = _in_proj(x2, lw["w_in16"], _TM_PROMPT)
    sh = lambda t: t.reshape(B, T, -1)
    vt16 = jnp.swapaxes(sh(v16), 1, 2)
    att = _prompt_attention(sh(q16), sh(k16), vt16, rel_bias, lw["lam_qk"], lw["attn_norm_g"], _TQ_PROMPT)
    p3 = sh(p)
    shift0 = jnp.zeros((B, RWKV_PROJ), F32)
    S0 = jnp.zeros((B, RWKV_HEADS, RWKV_HD, RWKV_HD), F32)
    o, g, bonus, S_new = _rwkv_fused(p3, shift0, lw["rwkv_params"], S0, _RWKV_CHUNK)
    fl = lambda t: t.reshape(n, -1)
    y = _layer_tail(gates, fl(att), fl(o), fl(bonus), fl(g), x2, lw, _TM_PROMPT, _PEER_TP, _PEER_TT)
    return (y.reshape(B, T, D_MODEL), k32.reshape(B, T, ATT_HEADS, 2 * ATT_HD), v32.reshape(B, T, ATT_HEADS, ATT_VD),
            S_new, p3[:, -1, :])


def _sample_layer(x, lw, rel_bias, cache_k, cache_v, layer, page_table, shift0, S0):
    DB, S, _ = x.shape
    n = DB * S
    x2 = x.reshape(n, D_MODEL)
    q16, k32, k16, v32, v16, gates, p = _in_proj(x2, lw["w_in16"], n)
    sh = lambda t: t.reshape(DB, S, -1)
    att = _sample_attention(sh(q16), sh(k32), sh(v32), cache_k, cache_v, layer, page_table, rel_bias,
                            lw["lam_qk"], lw["attn_norm_g"], _PAGES_PER_STEP)
    p3 = sh(p)
    prev = jnp.concatenate([shift0[:, None, :], p3[:, :-1, :]], axis=1).reshape(n, RWKV_PROJ)
    r, k, v, lgw, kk, b, g, bonus = _rwkv_prep_sample(p, prev, lw["rwkv_params"])
    pad = lambda t: jnp.pad(sh(t), ((0, 0), (0, _RWKV_CHUNK_SAMPLE - S), (0, 0)))
    o, S_new = _rwkv_chunked(pad(r), pad(k), pad(v), pad(lgw), pad(kk), pad(b), S0, _RWKV_CHUNK_SAMPLE)
    o2 = o[:, :S, :].reshape(n, RWKV_W)
    y = _layer_tail(gates, att.reshape(n, ATT_V), o2, bonus, g, x2, lw, n, n, n)
    return (y.reshape(DB, S, D_MODEL), k32.reshape(DB, S, ATT_HEADS, 2 * ATT_HD), v32.reshape(DB, S, ATT_HEADS, ATT_VD),
            S_new, p3[:, -1, :])


def kernel(x_prompt, x_sample, cache_k, cache_v, state_wkv, state_shift, page_table, w_in, w_out, lam_qk, attn_norm_g, shift_mu, rwkv_w0, rwkv_w2, rwkv_a0, rwkv_a2, rwkv_g2, rwkv_kk, rwkv_ka, rwkv_rk, rwkv_ln_g, rwkv_ln_b, ln1_g, ln1_b, ln2_g, ln2_b, peer_wq, peer_keys, peer_u, peer_v, rel_bias):
    assert w_in.shape[0] == DEPTH == 1
    yp, ys = x_prompt, x_sample
    outs = [[] for _ in range(8)]
    for l in range(DEPTH):
        lw = {
            "w_in16": w_in[l].astype(BF16), "wout16": w_out[l].astype(BF16),
            "lam_qk": lam_qk[l], "attn_norm_g": attn_norm_g[l],
            "rwkv_params": _rwkv_param_list(shift_mu[l], rwkv_w0[l], rwkv_w2[l], rwkv_a0[l], rwkv_a2[l],
                                            rwkv_g2[l], rwkv_kk[l], rwkv_ka[l], rwkv_rk[l]),
            "rwkv_ln_g": rwkv_ln_g[l], "rwkv_ln_b": rwkv_ln_b[l],
            "ln1_g": ln1_g[l], "ln1_b": ln1_b[l], "ln2_g": ln2_g[l], "ln2_b": ln2_b[l],
            "wq16": peer_wq[l].astype(BF16),
            "keys16": peer_keys[l].reshape(PEER_HEADS * 2, N_KEYS, PEER_HALF).astype(BF16),
            "u16": peer_u[l].astype(BF16),
            "vt16": jnp.swapaxes(peer_v[l].astype(BF16).reshape(-1, _PEER_I1_PER_BLOCK * N_KEYS, D_MODEL), 1, 2),
        }
        yp, kp, vp, Sp, shp = _prompt_layer(yp, lw, rel_bias)
        ys, ksm, vsm, Ss, shs = _sample_layer(ys, lw, rel_bias, cache_k, cache_v, l, page_table,
                                              state_shift[l], state_wkv[l])
        for lst, val in zip(outs, (kp, vp, ksm, vsm, Sp, Ss, shp, shs)):
            lst.append(val)
    stacked = tuple(jnp.stack(lst, axis=0) for lst in outs)
    return (yp, ys) + stacked
```

```python
import functools
import math

import jax
import jax.numpy as jnp
from jax import lax
from jax.experimental import pallas as pl
from jax.experimental.pallas import tpu as pltpu

D_MODEL = 1024
PAGE_SIZE = 128
ATT_HD = 64
ATT_HEADS = D_MODEL // (2 * ATT_HD)
ATT_VD = 2 * ATT_HD
ATT_QK = ATT_HEADS * 2 * ATT_HD
ATT_V = ATT_HEADS * ATT_VD
N_BUCKETS = 32
MAX_DISTANCE = 128
ATT_EPS = 1e-5
RWKV_HD = 64
RWKV_HEADS = D_MODEL // RWKV_HD
RWKV_W = RWKV_HEADS * RWKV_HD
DECAY_LORA = 64
AAA_LORA = 64
GATE_LORA = 128
RWKV_PROJ = 3 * RWKV_W + DECAY_LORA + AAA_LORA + GATE_LORA
DECAY_SCALE = 0.606531
RWKV_GN_EPS = 64e-5
O_K = ATT_QK
O_V = O_K + ATT_QK
O_GA = O_V + ATT_V
O_GR = O_GA + D_MODEL
O_RW = O_GR + D_MODEL
N_IN = O_RW + RWKV_PROJ
PEER_HEADS = 8
N_KEYS = 128
N_EXPERTS = N_KEYS * N_KEYS
PEER_QDIM = 256
PEER_HALF = PEER_QDIM // 2
PEER_TOPK = 16
DEPTH = 1
DN_ALPHA = (2.0 * DEPTH) ** 0.25
LN_EPS = 1e-5
LAMBDA_INIT = 0.8 - 0.6 * math.exp(-0.3 * 0)

V7X_LANES = 128
V7X_SUBLANES = 8
V7X_VMEM_LIMIT_BYTES = 56 * 1024 * 1024

NEG_INF = -1e30
LOG2E = math.log2(math.e)
BF16 = jnp.bfloat16
F32 = jnp.float32

NT_DIMS = (((1,), (1,)), ((), ()))
TN_DIMS = (((0,), (0,)), ((), ()))


def _cparams(*sem, flags=None):
    return pltpu.CompilerParams(dimension_semantics=sem, vmem_limit_bytes=V7X_VMEM_LIMIT_BYTES, flags=flags)


def _dot(a, b):
    return jnp.dot(a, b, preferred_element_type=F32)


def _dot_nt(a, b):
    return lax.dot_general(a, b, NT_DIMS, preferred_element_type=F32)


def _dot_tn(a, b):
    return lax.dot_general(a, b, TN_DIMS, preferred_element_type=F32)


def _split_dot(x, w_bf16):
    hi = x.astype(BF16)
    lo = (x - hi.astype(F32)).astype(BF16)
    return _dot(hi, w_bf16) + _dot(lo, w_bf16)


_P_CHUNKS = ((0, 1024), (1024, 2048), (2048, 3072), (3072, RWKV_PROJ))


def _in_proj_kernel(x_ref, w_ref, q16_ref, k32_ref, k16_ref, v32_ref, v16_ref, gates_ref, p_ref):
    x = x_ref[...].astype(BF16)

    def mm(lo, hi):
        return _dot(x, w_ref[:, lo:hi])

    q16_ref[...] = (mm(0, O_K) * (ATT_HD ** -0.5 * LOG2E)).astype(BF16)
    k = mm(O_K, O_V)
    k32_ref[...] = k
    k16_ref[...] = k.astype(BF16)
    v = mm(O_V, O_GA)
    v32_ref[...] = v
    v16_ref[...] = v.astype(BF16)
    gates_ref[:, 0:D_MODEL] = mm(O_GA, O_GR)
    gates_ref[:, D_MODEL:2 * D_MODEL] = mm(O_GR, O_RW)
    for lo, hi in _P_CHUNKS:
        p_ref[:, lo:hi] = mm(O_RW + lo, O_RW + hi)


def _in_proj(x2d, w_bf16, tm):
    n = x2d.shape[0]
    row = lambda i: (i, 0)
    full = lambda i: (0, 0)
    out_shape = (
        jax.ShapeDtypeStruct((n, ATT_QK), BF16),
        jax.ShapeDtypeStruct((n, ATT_QK), F32),
        jax.ShapeDtypeStruct((n, ATT_QK), BF16),
        jax.ShapeDtypeStruct((n, ATT_V), F32),
        jax.ShapeDtypeStruct((n, ATT_V), BF16),
        jax.ShapeDtypeStruct((n, 2 * D_MODEL), F32),
        jax.ShapeDtypeStruct((n, RWKV_PROJ), F32),
    )
    out_specs = tuple(pl.BlockSpec((tm, s.shape[1]), row) for s in out_shape)
    return pl.pallas_call(
        _in_proj_kernel,
        grid=(n // tm,),
        in_specs=[pl.BlockSpec((tm, D_MODEL), row),
                  pl.BlockSpec((D_MODEL, N_IN), full, pipeline_mode=pl.Buffered(1))],
        out_specs=out_specs,
        out_shape=out_shape,
        compiler_params=_cparams("parallel"),
        name="in_proj",
    )(x2d, w_bf16)


def _t5_bucket(rel):
    n = jnp.maximum(rel, 0)
    max_exact = N_BUCKETS // 2
    nf = jnp.maximum(n, 1).astype(F32)
    large = max_exact + (jnp.log(nf / max_exact) / math.log(MAX_DISTANCE / max_exact)
                         * (N_BUCKETS - max_exact)).astype(jnp.int32)
    large = jnp.minimum(large, N_BUCKETS - 1)
    return jnp.where(n < max_exact, n, large)


def _shifted_bias(rel_bias, dist):
    far = lax.dynamic_index_in_dim(rel_bias, _t5_bucket(jnp.int32(2 * MAX_DISTANCE)), 0, keepdims=False)
    table = (rel_bias - far[None, :]) * LOG2E
    bucket = _t5_bucket(dist)[None]
    expand = (slice(None),) + (None,) * dist.ndim
    out = jnp.zeros((rel_bias.shape[1],) + dist.shape, F32)
    for b in range(N_BUCKETS):
        out = jnp.where(bucket == b, table[b][expand], out)
    return out


def _per_row_head(tiles, row_h):
    out = jnp.zeros(tiles.shape[1:], F32)
    for h in range(tiles.shape[0]):
        out = jnp.where(row_h[:, None] == h, tiles[h], out)
    return out


def _lambda_value(lq):
    s01 = jnp.sum(lq[0:1] * lq[1:2], axis=-1, keepdims=True)
    s23 = jnp.sum(lq[2:3] * lq[3:4], axis=-1, keepdims=True)
    return jnp.exp(s01) - jnp.exp(s23) + LAMBDA_INIT


def _rms_scale(o):
    return o * lax.rsqrt(jnp.mean(o * o, axis=-1, keepdims=True) + ATT_EPS)


def _prompt_attn_kernel(q_ref, k_ref, vt_ref, bias_ref, lam_ref, g_ref, o_ref, m_ref, l_ref, acc_ref,
                        sa_ref, sb_ref, *, tq):
    i = pl.program_id(2)
    tk = tq // 2
    lane = lax.broadcasted_iota(jnp.int32, (tq, 2 * ATT_HD), 1)
    q = q_ref[0]
    zero = jnp.zeros_like(q)
    qm = (jnp.where(lane < ATT_HD, q, zero), jnp.where(lane >= ATT_HD, q, zero))

    m_ref[...] = jnp.full(m_ref.shape, NEG_INF, F32)
    l_ref[...] = jnp.zeros(l_ref.shape, F32)
    acc_ref[...] = jnp.zeros(acc_ref.shape, F32)

    def scores(b, s_ref):
        kb = k_ref[0, pl.ds(pl.multiple_of(b * tk, tk), tk), :]
        for c in range(2):
            s_ref[c] = _dot_nt(kb, qm[c])

    def consume(b, s_ref, add_t):
        vt = vt_ref[0, :, pl.ds(pl.multiple_of(b * tk, tk), tk)]
        p16, alphas = [], []
        for c in range(2):
            sc = s_ref[c] if add_t is None else s_ref[c] + add_t
            m_old = m_ref[c]
            m_new = jnp.maximum(m_old, jnp.max(sc, axis=0, keepdims=True))
            alpha = jnp.exp2(m_old - m_new)
            p = jnp.exp2(sc - m_new)
            l_ref[c] = alpha * l_ref[c] + jnp.sum(p, axis=0, keepdims=True)
            m_ref[c] = m_new
            p16.append(p.astype(BF16))
            alphas.append(alpha)
        for c in range(2):
            acc_ref[c] = alphas[c] * acc_ref[c] + _dot(vt, p16[c])

    scores(0, sa_ref)

    def far_body(p, carry):
        scores(2 * p + 1, sb_ref)
        consume(2 * p, sa_ref, None)
        scores(2 * p + 2, sa_ref)
        consume(2 * p + 1, sb_ref, None)
        return carry

    lax.fori_loop(0, jnp.maximum(i - 1, 0), far_body, 0)

    @pl.when(i > 0)
    def _():
        scores(2 * i - 1, sb_ref)
        consume(2 * i - 2, sa_ref, None)
        scores(2 * i, sa_ref)
        consume(2 * i - 1, sb_ref, bias_ref[0, 0])

    scores(2 * i + 1, sb_ref)
    consume(2 * i, sa_ref, bias_ref[0, 1])
    consume(2 * i + 1, sb_ref, bias_ref[0, 2])

    lam = _lambda_value(lam_ref[...])
    ot = acc_ref[0] / l_ref[0] - lam * (acc_ref[1] / l_ref[1])
    ot = ot * lax.rsqrt(jnp.mean(ot * ot, axis=0, keepdims=True) + ATT_EPS)
    o_ref[0] = ot.T * (g_ref[...] * (1.0 - LAMBDA_INIT))


def _prompt_attention(q16, k16, vt16, rel_bias, lam_qk, attn_norm_g, tq):
    B, T, _ = q16.shape
    tk = tq // 2
    assert tk >= MAX_DISTANCE
    diff_t = jnp.arange(tq, dtype=jnp.int32)[None, :] - jnp.arange(tk, dtype=jnp.int32)[:, None]
    tiles = []
    for off in (tk, 0, -tk):
        dist = diff_t + off
        tiles.append(jnp.where(dist >= 0, _shifted_bias(rel_bias, dist), NEG_INF))
    bias = jnp.stack(tiles, axis=1).astype(F32)
    kern = functools.partial(_prompt_attn_kernel, tq=tq)
    return pl.pallas_call(
        kern,
        grid=(B, ATT_HEADS, T // tq),
        in_specs=[
            pl.BlockSpec((1, tq, 2 * ATT_HD), lambda b, h, i: (b, i, h)),
            pl.BlockSpec((1, T, 2 * ATT_HD), lambda b, h, i: (b, 0, h)),
            pl.BlockSpec((1, ATT_VD, T), lambda b, h, i: (b, h, 0)),
            pl.BlockSpec((1, 3, tk, tq), lambda b, h, i: (h, 0, 0, 0)),
            pl.BlockSpec((4, ATT_HD), lambda b, h, i: (0, 0)),
            pl.BlockSpec((1, ATT_VD), lambda b, h, i: (0, h)),
        ],
        out_specs=pl.BlockSpec((1, tq, ATT_VD), lambda b, h, i: (b, i, h)),
        out_shape=jax.ShapeDtypeStruct((B, T, ATT_V), F32),
        scratch_shapes=[pltpu.VMEM((2, 1, tq), F32), pltpu.VMEM((2, 1, tq), F32),
                        pltpu.VMEM((2, ATT_VD, tq), F32),
                        pltpu.VMEM((2, tk, tq), F32), pltpu.VMEM((2, tk, tq), F32)],
        compiler_params=_cparams("parallel", "parallel", "arbitrary"),
        name="prompt_attention",
    )(q16, k16, vt16, bias, lam_qk, attn_norm_g.reshape(1, ATT_V))


def _sample_attn_kernel(pt_ref, q_ref, plain_ref, last_ref, new_ref, knew_ref, vnew_ref, lam_ref, g_ref,
                        *rest, pages_per_step):
    k_refs = rest[:pages_per_step]
    v_refs = rest[pages_per_step:2 * pages_per_step]
    o_ref, m_ref, l_ref, acc_ref = rest[2 * pages_per_step:]
    j = pl.program_id(1)
    last = pl.num_programs(1) - 1
    q = q_ref[0]

    @pl.when(j == 0)
    def _():
        m_ref[...] = jnp.full(m_ref.shape, NEG_INF, F32)
        l_ref[...] = jnp.zeros(l_ref.shape, F32)
        acc_ref[...] = jnp.zeros(acc_ref.shape, F32)

    def update(s, v2d):
        m_old = m_ref[...]
        m_new = jnp.maximum(m_old, jnp.max(s, axis=-1, keepdims=True))
        alpha = jnp.exp2(m_old - m_new)
        p = jnp.exp2(s - m_new)
        l_ref[...] = alpha * l_ref[...] + jnp.sum(p, axis=-1, keepdims=True)
        acc_ref[...] = alpha * acc_ref[...] + _dot(p.astype(BF16), v2d)
        m_ref[...] = m_new

    rows2d = PAGE_SIZE * ATT_HEADS
    k_all = jnp.concatenate([r[...].reshape(rows2d, 2 * ATT_HD).astype(BF16) for r in k_refs], axis=0)
    v_all = jnp.concatenate([r[...].reshape(rows2d, ATT_VD).astype(BF16) for r in v_refs], axis=0)
    add = jnp.concatenate([plain_ref[...]] * (pages_per_step - 1) + [last_ref[0]], axis=1)
    update(_dot_nt(q, k_all) + add, v_all)

    @pl.when(j == last)
    def _():
        n_new = knew_ref.shape[1]
        k2d = knew_ref[0].reshape(n_new * ATT_HEADS, 2 * ATT_HD).astype(BF16)
        v2d = vnew_ref[0].reshape(n_new * ATT_HEADS, ATT_VD).astype(BF16)
        update(_dot_nt(q, k2d) + new_ref[...], v2d)
        half = acc_ref.shape[0] // 2
        o = acc_ref[...] / l_ref[...]
        lam = _lambda_value(lam_ref[...])
        diff = o[0:half] - lam * o[half:2 * half]
        o_ref[0] = _rms_scale(diff) * (g_ref[...] * (1.0 - LAMBDA_INIT))


def _sample_attention(q16, k32, v32, cache_k, cache_v, layer, page_table, rel_bias, lam_qk, attn_norm_g,
                      pages_per_step):
    DB, S, _ = q16.shape
    n_pages = page_table.shape[1]
    past = n_pages * PAGE_SIZE
    H = ATT_HEADS
    rows = 2 * S * H
    q5 = q16.reshape(DB, S, H, 2, ATT_HD)
    qrows = jnp.stack([jnp.concatenate([q5[:, :, :, 0], jnp.zeros_like(q5[:, :, :, 0])], axis=-1),
                       jnp.concatenate([jnp.zeros_like(q5[:, :, :, 1]), q5[:, :, :, 1]], axis=-1)], axis=1)
    qrows = qrows.reshape(DB, rows, 2 * ATT_HD)
    row_s = (jnp.arange(rows, dtype=jnp.int32) // H) % S
    row_h = jnp.arange(rows, dtype=jnp.int32) % H
    col_t = jnp.arange(PAGE_SIZE * H, dtype=jnp.int32) // H
    col_h = jnp.arange(PAGE_SIZE * H, dtype=jnp.int32) % H
    same_head = row_h[:, None] == col_h[None, :]
    plain = jnp.where(same_head, 0.0, NEG_INF).astype(F32)
    dist_last = PAGE_SIZE + row_s[:, None] - col_t[None, :]
    b_last = _per_row_head(_shifted_bias(rel_bias, dist_last), row_h)
    last_tile = jnp.where(same_head, b_last, NEG_INF).astype(F32)
    tiles = jnp.stack([plain, last_tile], axis=0)
    ncol_t = jnp.arange(S * H, dtype=jnp.int32) // H
    ncol_h = jnp.arange(S * H, dtype=jnp.int32) % H
    dist_new = row_s[:, None] - ncol_t[None, :]
    b_new = _per_row_head(_shifted_bias(rel_bias, dist_new), row_h)
    ok_new = (row_h[:, None] == ncol_h[None, :]) & (dist_new >= 0)
    new_tile = jnp.where(ok_new, b_new, NEG_INF).astype(F32)
    g_tile = jnp.tile(attn_norm_g.reshape(H, ATT_VD), (S, 1))

    assert n_pages % pages_per_step == 0
    n_steps = n_pages // pages_per_step
    kern = functools.partial(_sample_attn_kernel, pages_per_step=pages_per_step)

    def page_spec(g):
        return pl.BlockSpec((None, None, PAGE_SIZE, H, 2 * ATT_HD),
                            lambda b, j, pt, g=g: (layer, pt[b, j * pages_per_step + g], 0, 0, 0))

    grid_spec = pltpu.PrefetchScalarGridSpec(
        num_scalar_prefetch=1,
        grid=(DB, n_steps),
        in_specs=[
            pl.BlockSpec((1, rows, 2 * ATT_HD), lambda b, j, pt: (b, 0, 0)),
            pl.BlockSpec((None, rows, PAGE_SIZE * H), lambda b, j, pt: (0, 0, 0)),
            pl.BlockSpec((1, rows, PAGE_SIZE * H), lambda b, j, pt: (jnp.where(j == n_steps - 1, 1, 0), 0, 0)),
            pl.BlockSpec((rows, S * H), lambda b, j, pt: (0, 0)),
            pl.BlockSpec((1, S, H, 2 * ATT_HD), lambda b, j, pt: (b, 0, 0, 0)),
            pl.BlockSpec((1, S, H, ATT_VD), lambda b, j, pt: (b, 0, 0, 0)),
            pl.BlockSpec((4, ATT_HD), lambda b, j, pt: (0, 0)),
            pl.BlockSpec((S * H, ATT_VD), lambda b, j, pt: (0, 0)),
        ] + [page_spec(g) for g in range(pages_per_step)] + [page_spec(g) for g in range(pages_per_step)],
        out_specs=pl.BlockSpec((1, S * H, ATT_VD), lambda b, j, pt: (b, 0, 0)),
        scratch_shapes=[pltpu.VMEM((rows, 1), F32), pltpu.VMEM((rows, 1), F32), pltpu.VMEM((rows, ATT_VD), F32)],
    )
    out = pl.pallas_call(
        kern,
        grid_spec=grid_spec,
        out_shape=jax.ShapeDtypeStruct((DB, S * H, ATT_VD), F32),
        compiler_params=_cparams("parallel", "arbitrary"),
        name="sample_attention",
    )(page_table, qrows, tiles, tiles, new_tile,
      k32.reshape(DB, S, H, 2 * ATT_HD), v32.reshape(DB, S, H, ATT_VD), lam_qk, g_tile,
      *([cache_k] * pages_per_step), *([cache_v] * pages_per_step))
    return out.reshape(DB, S, ATT_V)


def _head_ones(n, seg):
    r = lax.broadcasted_iota(jnp.int32, (n, n), 0) // seg
    c = lax.broadcasted_iota(jnp.int32, (n, n), 1) // seg
    return jnp.where(r == c, 1.0, 0.0).astype(BF16)


def _segsum(x, ones_bd):
    outs = []
    for g in range(x.shape[1] // V7X_LANES):
        outs.append(_split_dot(x[:, g * V7X_LANES:(g + 1) * V7X_LANES], ones_bd))
    return jnp.concatenate(outs, axis=1)


def _rwkv_prep_core(p, prev, mu_ref, w0_ref, w2_ref, a0_ref, a2_ref, g2_ref, kks_ref, ka_ref, rk_ref, outs):
    r_ref, k_ref, v_ref, lw_ref, kk_ref, b_ref, g_ref, bonus_ref = outs
    W = RWKV_W
    xs = p + (prev - p) * mu_ref[...]
    r = xs[:, 0:W]
    k = xs[:, W:2 * W]
    v = xs[:, 2 * W:3 * W]
    wa = xs[:, 3 * W:3 * W + DECAY_LORA + AAA_LORA]
    gd = xs[:, 3 * W + DECAY_LORA + AAA_LORA:RWKV_PROJ]
    logw = -DECAY_SCALE * jax.nn.sigmoid(w0_ref[...] + _dot(jnp.tanh(wa).astype(BF16), w2_ref[...]))
    a = jax.nn.sigmoid(a0_ref[...] + _dot(wa.astype(BF16), a2_ref[...]))
    g = _dot(jax.nn.sigmoid(gd).astype(BF16), g2_ref[...])
    ones_bd = _head_ones(V7X_LANES, RWKV_HD)
    kk = k * kks_ref[...]
    kk = kk / jnp.maximum(jnp.sqrt(_segsum(kk * kk, ones_bd)), 1e-12)
    k = k * (1.0 + (a - 1.0) * ka_ref[...])
    r_ref[...] = r
    k_ref[...] = k
    v_ref[...] = v
    lw_ref[...] = logw
    kk_ref[...] = kk
    b_ref[...] = kk * a
    g_ref[...] = g
    bonus_ref[...] = _segsum(r * k * rk_ref[...], ones_bd) * v


def _rwkv_prep_sample_kernel(p_ref, prev_ref, *rest):
    params, outs = rest[:9], rest[9:]
    _rwkv_prep_core(p_ref[...], prev_ref[...], *params, outs)


def _rwkv_param_list(shift_mu, rwkv_w0, rwkv_w2, rwkv_a0, rwkv_a2, rwkv_g2, rwkv_kk, rwkv_ka, rwkv_rk):
    zeros = jnp.zeros((AAA_LORA, RWKV_W), F32)
    w2p = jnp.concatenate([rwkv_w2, zeros], axis=0).astype(BF16)
    a2p = jnp.concatenate([jnp.zeros((DECAY_LORA, RWKV_W), F32), rwkv_a2], axis=0).astype(BF16)
    row = lambda t: t.reshape(1, -1)
    return [row(shift_mu), row(rwkv_w0), w2p, row(rwkv_a0), a2p, rwkv_g2.astype(BF16),
            row(rwkv_kk), row(rwkv_ka), row(rwkv_rk)]


def _rwkv_prep_sample(p2, prev2, params):
    n = p2.shape[0]
    const = lambda i: (0, 0)
    in_specs = [pl.BlockSpec((n, RWKV_PROJ), const), pl.BlockSpec((n, RWKV_PROJ), const)] + [
        pl.BlockSpec(t.shape, const) for t in params]
    out_shape = tuple(jax.ShapeDtypeStruct((n, RWKV_W), F32) for _ in range(8))
    out_specs = tuple(pl.BlockSpec((n, RWKV_W), const) for _ in range(8))
    return pl.pallas_call(
        _rwkv_prep_sample_kernel,
        grid=(1,),
        in_specs=in_specs, out_specs=out_specs, out_shape=out_shape,
        compiler_params=_cparams("arbitrary"),
        name="rwkv_prep_sample",
    )(p2, prev2, *params)


def _rwkv_chunk_kernel(r_ref, k_ref, v_ref, lw_ref, kk_ref, b_ref, s0_ref, o_ref, sout_ref, s_scr, *, chunk):
    _rwkv_chunk_core(r_ref[0], k_ref[0], v_ref[0], lw_ref[0], kk_ref[0], b_ref[0], s0_ref, o_ref, sout_ref, s_scr,
                     chunk)


def _rwkv_fused_kernel(p_ref, tail_ref, shift_ref, *rest, chunk):
    params, rest = rest[:9], rest[9:]
    s0_ref, o_ref, g_ref, bonus_ref, sout_ref, s_scr = rest[:6]
    r_s, k_s, v_s, lw_s, kk_s, b_s = rest[6:]
    c_idx = pl.program_id(1)
    p = p_ref[0]
    first = jnp.where(c_idx == 0, shift_ref[0], tail_ref[0, V7X_SUBLANES - 1:V7X_SUBLANES, :])
    row = lax.broadcasted_iota(jnp.int32, p.shape, 0)
    prev = jnp.where(row == 0, first, pltpu.roll(p, 1, 0))
    _rwkv_prep_core(p, prev, *params, [r_s, k_s, v_s, lw_s, kk_s, b_s, g_ref.at[0], bonus_ref.at[0]])
    _rwkv_chunk_core(r_s[...], k_s[...], v_s[...], lw_s[...], kk_s[...], b_s[...], s0_ref, o_ref, sout_ref, s_scr,
                     chunk)


def _rwkv_chunk_core(r_in, k_in, v_in, lw, kk_in, b_in, s0_ref, o_ref, sout_ref, s_scr, chunk):
    C = chunk
    c_idx = pl.program_id(1)
    n_pairs = RWKV_HEADS // 2
    L = V7X_LANES

    @pl.when(c_idx == 0)
    def _():
        s_scr[...] = s0_ref[0]

    ti = lax.broadcasted_iota(jnp.int32, (C, C), 0)
    si = lax.broadcasted_iota(jnp.int32, (C, C), 1)
    tri = jnp.where(si <= ti, 1.0, 0.0).astype(BF16)
    hi = lw.astype(BF16)
    rem = lw - hi.astype(F32)
    mid = rem.astype(BF16)
    lo = (rem - mid.astype(F32)).astype(BF16)
    cs = _dot(tri, hi) + _dot(tri, mid) + _dot(tri, lo)
    g = jnp.exp(cs)
    gprev = jnp.exp(cs - lw)
    ginv = jnp.exp(-cs)
    kap_all = kk_in * gprev
    rt_all = r_in * g
    kt_all = k_in * ginv
    bt_all = b_in * ginv
    v_all = v_in
    g_last = g[C - 1:C, :]

    lane = lax.broadcasted_iota(jnp.int32, (C, L), 1)
    in_h0 = lane < RWKV_HD

    def stack(x):
        z = jnp.zeros_like(x)
        return jnp.concatenate([jnp.where(in_h0, x, z), jnp.where(in_h0, z, x)], axis=0)

    rr = lax.broadcasted_iota(jnp.int32, (2 * C, 2 * C), 0)
    cc = lax.broadcasted_iota(jnp.int32, (2 * C, 2 * C), 1)
    same = (rr // C) == (cc // C)
    strict = same & (cc < rr)
    incl = same & (cc <= rr)
    eye = jnp.where(rr == cc, 1.0, 0.0).astype(F32)
    n_sq = max(int(math.ceil(math.log2(C))) - 1, 0)

    prs = range(n_pairs)
    sls = [slice(pr * L, (pr + 1) * L) for pr in prs]
    vs16 = [stack(v_all[:, sl]).astype(BF16) for sl in sls]
    lhs = [jnp.concatenate([stack(kap_all[:, sl]).astype(BF16), stack(rt_all[:, sl]).astype(BF16)], axis=0)
           for sl in sls]
    rhs = [jnp.concatenate([stack(kt_all[:, sl]).astype(BF16), stack(bt_all[:, sl]).astype(BF16)], axis=0)
           for sl in sls]
    A = [_dot_nt(lhs[pr], rhs[pr]) for pr in prs]
    S = [s_scr[pr] for pr in prs]
    KS = [_dot_nt(lhs[pr], S[pr].astype(BF16)) for pr in prs]
    zero = jnp.zeros((2 * C, 2 * C), F32)
    Lk16 = [jnp.where(strict, A[pr][0:2 * C, 0:2 * C], zero).astype(BF16) for pr in prs]
    Lb = [jnp.where(strict, A[pr][0:2 * C, 2 * C:4 * C], zero) for pr in prs]
    MkMb = [jnp.concatenate([jnp.where(incl, A[pr][2 * C:4 * C, 0:2 * C], zero),
                             -jnp.where(incl, A[pr][2 * C:4 * C, 2 * C:4 * C], zero)], axis=1).astype(BF16)
            for pr in prs]
    LkV = [_dot(Lk16[pr], vs16[pr]) for pr in prs]
    X = [eye - Lb[pr] for pr in prs]
    P16 = [Lb[pr].astype(BF16) for pr in prs]
    for _ in range(n_sq):
        P16 = [_dot(P16[pr], P16[pr]).astype(BF16) for pr in prs]
        X = [X[pr] + _dot(X[pr].astype(BF16), P16[pr]) for pr in prs]
    U16 = [_dot(X[pr].astype(BF16), (KS[pr][0:2 * C] + LkV[pr]).astype(BF16)).astype(BF16) for pr in prs]
    O = [KS[pr][2 * C:4 * C] + _dot(MkMb[pr], jnp.concatenate([vs16[pr], U16[pr]], axis=0)) for pr in prs]
    upd = [_dot_tn(jnp.concatenate([vs16[pr], -U16[pr]], axis=0), rhs[pr]) for pr in prs]
    for pr in prs:
        o_ref[0, :, sls[pr]] = O[pr][0:C] + O[pr][C:2 * C]
        s_scr[pr] = (S[pr] + upd[pr]) * g_last[:, sls[pr]]

    @pl.when(c_idx == pl.num_programs(1) - 1)
    def _():
        sout_ref[0] = s_scr[...]


def _pair_state(S):
    B = S.shape[0]
    S = S.reshape(B, RWKV_HEADS // 2, 2, RWKV_HD, RWKV_HD)
    z = jnp.zeros_like(S[:, :, 0])
    top = jnp.concatenate([S[:, :, 0], z], axis=-1)
    bot = jnp.concatenate([z, S[:, :, 1]], axis=-1)
    return jnp.concatenate([top, bot], axis=-2)


def _unpair_state(Sbd):
    B = Sbd.shape[0]
    a = Sbd[:, :, :RWKV_HD, :RWKV_HD]
    b = Sbd[:, :, RWKV_HD:, RWKV_HD:]
    return jnp.stack([a, b], axis=2).reshape(B, RWKV_HEADS, RWKV_HD, RWKV_HD)


def _rwkv_fused(p3, shift0, params, S0, chunk):
    B, T, _ = p3.shape
    n_pairs = RWKV_HEADS // 2
    tail_blocks = chunk // V7X_SUBLANES
    const = lambda bb, c: (0, 0)
    tok = pl.BlockSpec((1, chunk, RWKV_W), lambda bb, c: (bb, c, 0))
    st = pl.BlockSpec((1, n_pairs, V7X_LANES, V7X_LANES), lambda bb, c: (bb, 0, 0, 0))
    in_specs = [
        pl.BlockSpec((1, chunk, RWKV_PROJ), lambda bb, c: (bb, c, 0)),
        pl.BlockSpec((1, V7X_SUBLANES, RWKV_PROJ), lambda bb, c: (bb, jnp.maximum(c * tail_blocks - 1, 0), 0)),
        pl.BlockSpec((1, 1, RWKV_PROJ), lambda bb, c: (bb, 0, 0)),
    ] + [pl.BlockSpec(t.shape, const) for t in params] + [st]
    tok_shape = jax.ShapeDtypeStruct((B, T, RWKV_W), F32)
    o, g, bonus, s_out = pl.pallas_call(
        functools.partial(_rwkv_fused_kernel, chunk=chunk),
        grid=(B, T // chunk),
        in_specs=in_specs,
        out_specs=(tok, tok, tok, st),
        out_shape=(tok_shape, tok_shape, tok_shape,
                   jax.ShapeDtypeStruct((B, n_pairs, V7X_LANES, V7X_LANES), F32)),
        scratch_shapes=[pltpu.VMEM((n_pairs, V7X_LANES, V7X_LANES), F32)] + [pltpu.VMEM((chunk, RWKV_W), F32)] * 6,
        compiler_params=_cparams("parallel", "arbitrary"),
        name="rwkv_fused_%d" % chunk,
    )(p3, p3, shift0.reshape(B, 1, RWKV_PROJ), *params, _pair_state(S0))
    return o, g, bonus, _unpair_state(s_out)


def _rwkv_chunked(r, k, v, lw, kk, b, S0, chunk):
    B, T, _ = r.shape
    n_pairs = RWKV_HEADS // 2
    tok = pl.BlockSpec((1, chunk, RWKV_W), lambda bb, c: (bb, c, 0))
    st = pl.BlockSpec((1, n_pairs, V7X_LANES, V7X_LANES), lambda bb, c: (bb, 0, 0, 0))
    o, s_out = pl.pallas_call(
        functools.partial(_rwkv_chunk_kernel, chunk=chunk),
        grid=(B, T // chunk),
        in_specs=[tok] * 6 + [st],
        out_specs=(tok, st),
        out_shape=(jax.ShapeDtypeStruct((B, T, RWKV_W), F32),
                   jax.ShapeDtypeStruct((B, n_pairs, V7X_LANES, V7X_LANES), F32)),
        scratch_shapes=[pltpu.VMEM((n_pairs, V7X_LANES, V7X_LANES), F32)],
        compiler_params=_cparams("parallel", "arbitrary"),
        name="rwkv_chunk_%d" % chunk,
    )(r, k, v, lw, kk, b, _pair_state(S0))
    return o, _unpair_state(s_out)


def _layer_norm(x, g, b):
    mu = jnp.mean(x, axis=-1, keepdims=True)
    var = jnp.mean(jnp.square(x - mu), axis=-1, keepdims=True)
    return (x - mu) * lax.rsqrt(var + LN_EPS) * g + b


def _mix_out_kernel(ga_ref, gr_ref, att_ref, o_ref, bonus_ref, g_ref, x_ref, wout_ref, gng_ref, gnb_ref,
                    ln_g_ref, ln_b_ref, h32_ref, h16_ref):
    ones_bd = _head_ones(V7X_LANES, RWKV_HD)
    o = o_ref[...]
    mean = _segsum(o, ones_bd) * (1.0 / RWKV_HD)
    d = o - mean
    var = _segsum(d * d, ones_bd) * (1.0 / RWKV_HD)
    rw = d * lax.rsqrt(var + RWKV_GN_EPS) * gng_ref[...] + gnb_ref[...]
    rw = (rw + bonus_ref[...]) * g_ref[...]
    mixed_in = jax.nn.sigmoid(ga_ref[...]) * att_ref[...] + jax.nn.sigmoid(gr_ref[...]) * rw
    mixed = _dot(mixed_in.astype(BF16), wout_ref[...])
    h = _layer_norm(DN_ALPHA * x_ref[...] + mixed, ln_g_ref[...], ln_b_ref[...])
    h32_ref[...] = h
    h16_ref[...] = h.astype(BF16)


def _mix_out(gates, att, o, bonus, g, x2d, wout16, gn_g, gn_b, ln_g, ln_b, tm):
    n = x2d.shape[0]
    row = lambda i: (i, 0)
    const = lambda i: (0, 0)
    tok = pl.BlockSpec((tm, D_MODEL), row)
    vec = pl.BlockSpec((1, D_MODEL), const)
    return pl.pallas_call(
        _mix_out_kernel,
        grid=(n // tm,),
        in_specs=[tok, pl.BlockSpec((tm, D_MODEL), lambda i: (i, 1)), tok, tok, tok, tok, tok,
                  pl.BlockSpec((D_MODEL, D_MODEL), const), vec, vec, vec, vec],
        out_specs=(tok, tok),
        out_shape=(jax.ShapeDtypeStruct((n, D_MODEL), F32), jax.ShapeDtypeStruct((n, D_MODEL), BF16)),
        compiler_params=_cparams("parallel"),
        name="mix_out",
    )(gates, gates, att, o, bonus, g, x2d, wout16, gn_g.reshape(1, -1), gn_b.reshape(1, -1),
      ln_g.reshape(1, -1), ln_b.reshape(1, -1))


_STAIR = tuple((p, q) for p in range(PEER_TOPK) for q in range(PEER_TOPK // (p + 1)))
_STAIR_ROWS = -(-len(_STAIR) // V7X_SUBLANES) * V7X_SUBLANES
_RANK_MARK0 = -3.0e38
_RANK_MARK_STEP = 1.0e36
_RANK_PAD = -2.9e38


def _top_ranks(x, k, exact):
    R = x.shape[0]
    row = lax.broadcasted_iota(jnp.int32, x.shape, 0)
    vals = []
    for it in range(k):
        m = jnp.max(x, axis=0, keepdims=True)
        if exact:
            hit = row == jnp.min(jnp.where(x == m, row, R), axis=0, keepdims=True)
        else:
            hit = x == m
        x = jnp.where(hit, _RANK_MARK0 - it * _RANK_MARK_STEP, x)
        vals.append(m)
    marked = x <= _RANK_MARK0
    rank = jnp.where(marked, jnp.floor((_RANK_MARK0 - x) * (1.0 / _RANK_MARK_STEP) + 0.5), float(k))
    count = jnp.sum(jnp.where(marked, 1.0, 0.0), axis=0, keepdims=True)
    return vals, rank, count


def _peer_select_kernel(h_ref, wq_ref, keys_ref, cut_ref, f1_ref, rank2_ref, e2_ref, cand_scr, sel_scr):
    q = _dot(h_ref[...], wq_ref[...]).astype(BF16)
    K = PEER_TOPK

    def select(exact):
        most = jnp.zeros((1, q.shape[0]), F32)
        for h in range(PEER_HEADS):
            s = []
            for c in range(2):
                blk = (h * 2 + c) * PEER_HALF
                s.append(_dot_nt(keys_ref[h * 2 + c], q[:, blk:blk + PEER_HALF]))
            a1, rank1, n1 = _top_ranks(s[0], K, exact)
            a2, rank2, n2 = _top_ranks(s[1], K, exact)
            cand_scr[...] = jnp.full(cand_scr.shape, _RANK_PAD, F32)
            for i, (p, qq) in enumerate(_STAIR):
                cand_scr[i:i + 1, :] = a1[p] + a2[qq]
            cand = cand_scr[...]
            _, crank, nc = _top_ranks(cand, K, exact)
            most = jnp.maximum(most, jnp.maximum(jnp.maximum(n1, n2), nc))
            chosen = crank < float(K)
            z = jnp.sum(jnp.where(chosen, jnp.exp(cand - (a1[0] + a2[0])), 0.0), axis=0, keepdims=True)
            sel_scr[...] = jnp.where(chosen, 1.0, 0.0)
            cut = jnp.zeros(rank1.shape, F32)
            start = 0
            for p in range(K):
                n_p = K // (p + 1)
                cut_p = jnp.sum(sel_scr[start:start + n_p, :], axis=0, keepdims=True)
                cut = jnp.where(rank1 == float(p), cut_p, cut)
                start += n_p
            cut_ref[h] = cut
            f1_ref[h] = jnp.exp(s[0] - a1[0]) * (0.5 / z)
            e2 = jnp.exp(s[1] - a2[0])
            group = rank2_ref.shape[-1]
            for g in range(rank2.shape[1] // group):
                cols = slice(g * group, (g + 1) * group)
                rank2_ref[h, g] = rank2[:, cols].reshape(_KEY_TILES, V7X_SUBLANES, group).astype(BF16)
                e2_ref[h, g] = e2[:, cols].reshape(_KEY_TILES, V7X_SUBLANES, group).astype(BF16)
        return most

    most = select(exact=False)

    @pl.when(jnp.max(most) > float(K))
    def _():
        select(exact=True)


_KEY_TILES = N_KEYS // V7X_SUBLANES


def _peer_group(n):
    return 2 * V7X_LANES if n % (2 * V7X_LANES) == 0 else V7X_LANES


def _peer_select(h16, wq16, keys16, tp):
    n = h16.shape[0]
    group = _peer_group(tp)
    sel_shape = jax.ShapeDtypeStruct((PEER_HEADS, N_KEYS, n), F32)
    sel_spec = pl.BlockSpec((PEER_HEADS, N_KEYS, tp), lambda i: (0, 0, i))
    tab_shape = jax.ShapeDtypeStruct((PEER_HEADS, n // group, _KEY_TILES, V7X_SUBLANES, group), BF16)
    tab_spec = pl.BlockSpec((PEER_HEADS, tp // group, _KEY_TILES, V7X_SUBLANES, group),
                            lambda i: (0, i, 0, 0, 0))
    return pl.pallas_call(
        _peer_select_kernel,
        grid=(n // tp,),
        in_specs=[pl.BlockSpec((tp, D_MODEL), lambda i: (i, 0)),
                  pl.BlockSpec(wq16.shape, lambda i: (0, 0)),
                  pl.BlockSpec(keys16.shape, lambda i: (0, 0, 0))],
        out_specs=(sel_spec, sel_spec, tab_spec, tab_spec),
        out_shape=(sel_shape, sel_shape, tab_shape, tab_shape),
        scratch_shapes=[pltpu.VMEM((_STAIR_ROWS, tp), F32), pltpu.VMEM((_STAIR_ROWS, tp), F32)],
        compiler_params=_cparams("parallel"),
        name="peer_select",
    )(h16, wq16, keys16)


_PEER_I1_PER_BLOCK = 8
_PEER_I1_PER_CHUNK = 2
_SQRT_HALF = math.sqrt(0.5)


def _peer_dense_kernel(x_ref, u_ref, vt_ref, cut_ref, f1_ref, rank2_ref, e2_ref, h_ref, ln_g_ref, ln_b_ref,
                       y_ref, acc_scr, hid_scr, a_scr):
    e = pl.program_id(1)

    @pl.when(e == 0)
    def _():
        acc_scr[...] = jnp.zeros(acc_scr.shape, F32)

    group = rank2_ref.shape[-1]
    n_groups = x_ref.shape[0] // group
    tile3 = (_KEY_TILES, V7X_SUBLANES, group)
    assert n_groups <= _PEER_I1_PER_BLOCK // _PEER_I1_PER_CHUNK

    def stages(cur, prev, with_hid=True, with_gates=True):
        def next_hid(tc):
            hid_scr[cur, tc] = _dot_nt(u_ref[...], x_ref[tc * group:(tc + 1) * group, :])

        if not with_gates:
            for tc in range(n_groups):
                next_hid(tc)
            return
        for ch in range(_PEER_I1_PER_BLOCK // _PEER_I1_PER_CHUNK):
            if with_hid and ch < n_groups:
                next_hid(ch)
            for il in range(ch * _PEER_I1_PER_CHUNK, (ch + 1) * _PEER_I1_PER_CHUNK):
                rows = slice(il * N_KEYS, (il + 1) * N_KEYS)
                for tc in range(n_groups):
                    cols = slice(tc * group, (tc + 1) * group)
                    w = jnp.zeros(tile3, BF16)
                    for h in range(PEER_HEADS):
                        cutb = jnp.broadcast_to(cut_ref[h, il:il + 1, cols], tile3[1:]).astype(BF16)
                        f1b = jnp.broadcast_to(f1_ref[h, il:il + 1, cols], tile3[1:]).astype(BF16)
                        e2 = e2_ref[h, tc]
                        w = w + jnp.where(rank2_ref[h, tc] < cutb[None], e2 * f1b[None], jnp.zeros_like(e2))
                    hb = hid_scr[prev, tc, rows, :]
                    act = (hb * (1.0 + lax.erf(hb * _SQRT_HALF))).reshape(tile3).astype(BF16)
                    a_scr[rows, cols] = (w * act).reshape(N_KEYS, group)
            crow = slice(ch * _PEER_I1_PER_CHUNK * N_KEYS, (ch + 1) * _PEER_I1_PER_CHUNK * N_KEYS)
            acc_scr[...] += _dot(vt_ref[:, crow], a_scr[crow, :])

    last = N_EXPERTS // (_PEER_I1_PER_BLOCK * N_KEYS)

    @pl.when(e == 0)
    def _():
        stages(0, 1, with_gates=False)

    for parity in range(2):
        @pl.when((e % 2 == parity) & (e > 0) & (e < last))
        def _(parity=parity):
            stages(parity, 1 - parity)

    @pl.when(e == last)
    def _():
        stages(last % 2, 1 - last % 2, with_hid=False)
        ff = acc_scr[...].T
        y_ref[...] = _layer_norm(DN_ALPHA * h_ref[...] + ff, ln_g_ref[...], ln_b_ref[...])


def _peer_dense(h32, h16, u16, vt16, sel, ln_g, ln_b, tt):
    n = h32.shape[0]
    cut, f1, rank2, e2 = sel
    group = rank2.shape[-1]
    assert tt % group == 0
    eb = _PEER_I1_PER_BLOCK * N_KEYS
    n_e = N_EXPERTS // eb
    tok = lambda i, e: (i, 0)
    const = lambda i, e: (0, 0)
    blk = lambda e, lag: jnp.clip(e - lag, 0, n_e - 1)
    tab_spec = pl.BlockSpec((PEER_HEADS, tt // group, _KEY_TILES, V7X_SUBLANES, group), lambda i, e: (0, i, 0, 0, 0))
    return pl.pallas_call(
        _peer_dense_kernel,
        grid=(n // tt, n_e + 1),
        in_specs=[pl.BlockSpec((tt, D_MODEL), tok),
                  pl.BlockSpec((eb, D_MODEL), lambda i, e: (blk(e, 0), 0)),
                  pl.BlockSpec((None, D_MODEL, eb), lambda i, e: (blk(e, 1), 0, 0)),
                  pl.BlockSpec((PEER_HEADS, _PEER_I1_PER_BLOCK, tt), lambda i, e: (0, blk(e, 1), i)),
                  pl.BlockSpec((PEER_HEADS, _PEER_I1_PER_BLOCK, tt), lambda i, e: (0, blk(e, 1), i)),
                  tab_spec, tab_spec,
                  pl.BlockSpec((tt, D_MODEL), tok),
                  pl.BlockSpec((1, D_MODEL), const), pl.BlockSpec((1, D_MODEL), const)],
        out_specs=pl.BlockSpec((tt, D_MODEL), tok),
        out_shape=jax.ShapeDtypeStruct((n, D_MODEL), F32),
        scratch_shapes=[pltpu.VMEM((D_MODEL, tt), F32), pltpu.VMEM((2, tt // group, eb, group), F32),
                        pltpu.VMEM((eb, tt), BF16)],
        compiler_params=_cparams("parallel", "arbitrary"),
        name="peer_dense",
    )(h16, u16, vt16, cut, f1, rank2, e2, h32, ln_g.reshape(1, -1), ln_b.reshape(1, -1))


_TM_PROMPT = 256
_TQ_PROMPT = 512
_RWKV_CHUNK = 64
_RWKV_CHUNK_SAMPLE = 16
_PAGES_PER_STEP = 16
_PEER_TP = 256
_PEER_TT = 512


def _layer_tail(gates, att2, o2, bonus2, g2, x2, lw, tm, tp, tt):
    h32, h16 = _mix_out(gates, att2, o2, bonus2, g2, x2, lw["wout16"], lw["rwkv_ln_g"], lw["rwkv_ln_b"],
                        lw["ln1_g"], lw["ln1_b"], tm)
    sel = _peer_select(h16, lw["wq16"], lw["keys16"], tp)
    return _peer_dense(h32, h16, lw["u16"], lw["vt16"], sel, lw["ln2_g"], lw["ln2_b"], tt)


def _prompt_layer(x, lw, rel_bias):
    B, T, _ = x.shape
    n = B * T
    x2 = x.reshape(n, D_MODEL)
    q16, k32, k16, v32, v16, gates, p = _in_proj(x2, lw["w_in16"], _TM_PROMPT)
    sh = lambda t: t.reshape(B, T, -1)
    vt16 = jnp.swapaxes(sh(v16), 1, 2)
    att = _prompt_attention(sh(q16), sh(k16), vt16, rel_bias, lw["lam_qk"], lw["attn_norm_g"], _TQ_PROMPT)
    p3 = sh(p)
    shift0 = jnp.zeros((B, RWKV_PROJ), F32)
    S0 = jnp.zeros((B, RWKV_HEADS, RWKV_HD, RWKV_HD), F32)
    o, g, bonus, S_new = _rwkv_fused(p3, shift0, lw["rwkv_params"], S0, _RWKV_CHUNK)
    fl = lambda t: t.reshape(n, -1)
    y = _layer_tail(gates, fl(att), fl(o), fl(bonus), fl(g), x2, lw, _TM_PROMPT, _PEER_TP, _PEER_TT)
    return (y.reshape(B, T, D_MODEL), k32.reshape(B, T, ATT_HEADS, 2 * ATT_HD), v32.reshape(B, T, ATT_HEADS, ATT_VD),
            S_new, p3[:, -1, :])


def _sample_layer(x, lw, rel_bias, cache_k, cache_v, layer, page_table, shift0, S0):
    DB, S, _ = x.shape
    n = DB * S
    x2 = x.reshape(n, D_MODEL)
    q16, k32, k16, v32, v16, gates, p = _in_proj(x2, lw["w_in16"], n)
    sh = lambda t: t.reshape(DB, S, -1)
    att = _sample_attention(sh(q16), sh(k32), sh(v32), cache_k, cache_v, layer, page_table, rel_bias,
                            lw["lam_qk"], lw["attn_norm_g"], _PAGES_PER_STEP)
    p3 = sh(p)
    prev = jnp.concatenate([shift0[:, None, :], p3[:, :-1, :]], axis=1).reshape(n, RWKV_PROJ)
    r, k, v, lgw, kk, b, g, bonus = _rwkv_prep_sample(p, prev, lw["rwkv_params"])
    pad = lambda t: jnp.pad(sh(t), ((0, 0), (0, _RWKV_CHUNK_SAMPLE - S), (0, 0)))
    o, S_new = _rwkv_chunked(pad(r), pad(k), pad(v), pad(lgw), pad(kk), pad(b), S0, _RWKV_CHUNK_SAMPLE)
    o2 = o[:, :S, :].reshape(n, RWKV_W)
    y = _layer_tail(gates, att.reshape(n, ATT_V), o2, bonus, g, x2, lw, n, n, n)
    return (y.reshape(DB, S, D_MODEL), k32.reshape(DB, S, ATT_HEADS, 2 * ATT_HD), v32.reshape(DB, S, ATT_HEADS, ATT_VD),
            S_new, p3[:, -1, :])


def kernel(x_prompt, x_sample, cache_k, cache_v, state_wkv, state_shift, page_table, w_in, w_out, lam_qk, attn_norm_g, shift_mu, rwkv_w0, rwkv_w2, rwkv_a0, rwkv_a2, rwkv_g2, rwkv_kk, rwkv_ka, rwkv_rk, rwkv_ln_g, rwkv_ln_b, ln1_g, ln1_b, ln2_g, ln2_b, peer_wq, peer_keys, peer_u, peer_v, rel_bias):
    assert w_in.shape[0] == DEPTH == 1
    yp, ys = x_prompt, x_sample
    outs = [[] for _ in range(8)]
    for l in range(DEPTH):
        lw = {
            "w_in16": w_in[l].astype(BF16), "wout16": w_out[l].astype(BF16),
            "lam_qk": lam_qk[l], "attn_norm_g": attn_norm_g[l],
            "rwkv_params": _rwkv_param_list(shift_mu[l], rwkv_w0[l], rwkv_w2[l], rwkv_a0[l], rwkv_a2[l],
                                            rwkv_g2[l], rwkv_kk[l], rwkv_ka[l], rwkv_rk[l]),
            "rwkv_ln_g": rwkv_ln_g[l], "rwkv_ln_b": rwkv_ln_b[l],
            "ln1_g": ln1_g[l], "ln1_b": ln1_b[l], "ln2_g": ln2_g[l], "ln2_b": ln2_b[l],
            "wq16": peer_wq[l].astype(BF16),
            "keys16": peer_keys[l].reshape(PEER_HEADS * 2, N_KEYS, PEER_HALF).astype(BF16),
            "u16": peer_u[l].astype(BF16),
            "vt16": jnp.swapaxes(peer_v[l].astype(BF16).reshape(-1, _PEER_I1_PER_BLOCK * N_KEYS, D_MODEL), 1, 2),
        }
        yp, kp, vp, Sp, shp = _prompt_layer(yp, lw, rel_bias)
        ys, ksm, vsm, Ss, shs = _sample_layer(ys, lw, rel_bias, cache_k, cache_v, l, page_table,
                                              state_shift[l], state_wkv[l])
        for lst, val in zip(outs, (kp, vp, ksm, vsm, Sp, Ss, shp, shs)):
            lst.append(val)
    stacked = tuple(jnp.stack(lst, axis=0) for lst in outs)
    return (yp, ys) + stacked
```
